```python
import math
import jax, jax.numpy as jnp
from jax import lax
import numpy as np

D_MODEL = 1024
BATCH = 2
SEQ = 16384
DEPTH = 2

N_META = 16
D_RNN = D_MODEL
RG_BLOCKS = 4
RG_BLOCK_W = D_RNN // RG_BLOCKS
CONV_W = 4
LRU_C = 8.0
N_HEADS = 8
HEAD_DIM = D_MODEL // N_HEADS
D_ATTN = N_HEADS * HEAD_DIM
IDX_HEADS = 8
IDX_DIM = 64
TOP_K_MAX = 256
Q_BLOCK = 128
N_BUCKETS = 32
MAX_DISTANCE = 128
D_FF = 4 * D_MODEL
NORM_EPS = 1e-6

SZ_RG_GATE = D_RNN
SZ_RG_X = D_RNN
SZ_Q = D_ATTN
SZ_K = D_ATTN
SZ_V = D_ATTN
SZ_QI = IDX_HEADS * IDX_DIM
SZ_KI = IDX_DIM
SZ_WI = IDX_HEADS
SZ_G_RNN = D_MODEL
SZ_G_ATTN = D_MODEL
IN_SIZES = (SZ_RG_GATE, SZ_RG_X, SZ_Q, SZ_K, SZ_V, SZ_QI, SZ_KI, SZ_WI, SZ_G_RNN, SZ_G_ATTN)
N_IN = sum(IN_SIZES)
SPLIT_POINTS = tuple(int(s) for s in np.cumsum(IN_SIZES)[:-1])

kernel_name = "hybrid_rglru_dsa_gated_block"


def rms_norm(x, g):
    xf = x.astype(jnp.float32)
    y = xf * lax.rsqrt(jnp.mean(xf * xf, axis=-1, keepdims=True) + NORM_EPS)
    return (y * g.astype(jnp.float32)).astype(x.dtype)


def causal_depthwise_conv(x, w, b):
    L = x.shape[1]
    xp = jnp.pad(x, ((0, 0), (CONV_W - 1, 0), (0, 0)))
    y = w[0] * xp[:, 0:L]
    for j in range(1, CONV_W):
        y = y + w[j] * xp[:, j:j + L]
    return y + b


def block_diag_linear(x, w, b):
    B, L, _ = x.shape
    xg = x.reshape(B, L, RG_BLOCKS, RG_BLOCK_W)
    return jnp.einsum('blgi,gij->blgj', xg, w).reshape(B, L, D_RNN) + b


def rg_lru(x, w_a, b_a, w_x, b_x, lam):
    r = jax.nn.sigmoid(block_diag_linear(x, w_a, b_a)).astype(jnp.float32)
    i = jax.nn.sigmoid(block_diag_linear(x, w_x, b_x)).astype(jnp.float32)
    log_a = -LRU_C * r * jax.nn.softplus(-lam.astype(jnp.float32))
    a = jnp.exp(log_a)
    u = jnp.sqrt(-jnp.expm1(2.0 * log_a)) * (i * x.astype(jnp.float32))

    def combine(c1, c2):
        a1, b1 = c1
        a2, b2 = c2
        return a1 * a2, a2 * b1 + b2

    _, h = lax.associative_scan(combine, (a, u), axis=1)
    return h.astype(x.dtype)


def t5_causal_bucket(dist):
    max_exact = N_BUCKETS // 2
    d = jnp.maximum(dist, 0)
    df = jnp.maximum(d, 1).astype(jnp.float32)
    large = max_exact + (jnp.log(df / max_exact) / math.log(MAX_DISTANCE / max_exact)
                         * (N_BUCKETS - max_exact)).astype(jnp.int32)
    large = jnp.minimum(large, N_BUCKETS - 1)
    return jnp.where(d < max_exact, d, large)


def sparse_indexed_attention(q, k, v, q_idx, k_idx, w_idx, rel_bias):
    B, L = q.shape[0], q.shape[1]
    top_k = min(TOP_K_MAX, L // 4)
    n_blocks = -(-L // Q_BLOCK)
    Lp = n_blocks * Q_BLOCK

    def pad(a):
        return jnp.pad(a, ((0, 0), (0, Lp - L)) + ((0, 0),) * (a.ndim - 2))

    qp, qip, wp = pad(q), pad(q_idx), pad(w_idx)
    key_pos = jnp.arange(L, dtype=jnp.int32)
    scale = HEAD_DIM ** -0.5
    gather = jax.vmap(lambda a, idx: a[idx])

    def one_block(blk):
        start = blk * Q_BLOCK
        qb = lax.dynamic_slice_in_dim(qp, start, Q_BLOCK, axis=1)
        qib = lax.dynamic_slice_in_dim(qip, start, Q_BLOCK, axis=1)
        wb = lax.dynamic_slice_in_dim(wp, start, Q_BLOCK, axis=1)
        t = start + jnp.arange(Q_BLOCK, dtype=jnp.int32)
        s_raw = jnp.einsum('bqhd,bkd->bqhk', qib, k_idx)
        score = jnp.einsum('bqhk,bqh->bqk', jax.nn.relu(s_raw), wb).astype(jnp.float32)
        causal = key_pos[None, None, :] <= t[None, :, None]
        score = jnp.where(causal, score, -jnp.inf)
        _, sel = lax.top_k(score, top_k)
        k_sel = gather(k, sel)
        v_sel = gather(v, sel)
        bias = rel_bias[t5_causal_bucket(t[None, :, None] - sel)]
        logits = (jnp.einsum('bqhd,bqkhd->bqhk', qb, k_sel).astype(jnp.float32) * scale
                  + jnp.moveaxis(bias, -1, 2).astype(jnp.float32))
        valid = (sel <= t[None, :, None])[:, :, None, :]
        logits = jnp.where(valid, logits, -jnp.inf)
        p = jax.nn.softmax(logits, axis=-1).astype(v.dtype)
        return jnp.einsum('bqhk,bqkhd->bqhd', p, v_sel)

    out = lax.map(one_block, jnp.arange(n_blocks, dtype=jnp.int32))
    out = jnp.moveaxis(out, 0, 1).reshape(B, Lp, N_HEADS, HEAD_DIM)[:, :L]
    return out


def hybrid_layer(h, norm1_g, w_in, conv_w, conv_b, w_rg_a, b_rg_a, w_rg_x, b_rg_x,
                 lru_lambda, w_out, norm2_g, w_mlp1, w_mlp2, rel_bias):
    B, L, _ = h.shape
    xn = rms_norm(h, norm1_g)
    proj = xn @ w_in
    (u_gate, u_rnn, q, k, v, qi, ki, wi, g_rnn, g_attn) = jnp.split(proj, SPLIT_POINTS, axis=-1)
    xc = causal_depthwise_conv(u_rnn, conv_w, conv_b)
    y_rnn = rg_lru(xc, w_rg_a, b_rg_a, w_rg_x, b_rg_x, lru_lambda) * jax.nn.gelu(u_gate)
    y_attn = sparse_indexed_attention(
        q.reshape(B, L, N_HEADS, HEAD_DIM), k.reshape(B, L, N_HEADS, HEAD_DIM),
        v.reshape(B, L, N_HEADS, HEAD_DIM), qi.reshape(B, L, IDX_HEADS, IDX_DIM),
        ki, wi, rel_bias).reshape(B, L, D_ATTN)
    y = jax.nn.sigmoid(g_rnn) * y_rnn + jax.nn.sigmoid(g_attn) * y_attn
    h = h + y @ w_out
    hn = rms_norm(h, norm2_g)
    h = h + jnp.square(jax.nn.relu(hn @ w_mlp1)) @ w_mlp2
    return h


def setup_inputs(seed: int = 0) -> dict:
    key = jax.random.key(seed)
    ks = jax.random.split(key, 20)
    f32 = jnp.float32
    x = jax.random.normal(ks[0], (BATCH, SEQ, D_MODEL), f32)
    norm1_g = 1.0 + 0.01 * jax.random.normal(ks[1], (DEPTH, D_MODEL), f32)
    w_in = jax.random.normal(ks[2], (DEPTH, D_MODEL, N_IN), f32) * D_MODEL ** -0.5
    conv_w = jax.random.normal(ks[3], (DEPTH, CONV_W, D_RNN), f32) * CONV_W ** -0.5
    conv_b = 0.01 * jax.random.normal(ks[4], (DEPTH, D_RNN), f32)
    w_rg_a = jax.random.normal(ks[5], (DEPTH, RG_BLOCKS, RG_BLOCK_W, RG_BLOCK_W), f32) * RG_BLOCK_W ** -0.5
    b_rg_a = 0.01 * jax.random.normal(ks[6], (DEPTH, D_RNN), f32)
    w_rg_x = jax.random.normal(ks[7], (DEPTH, RG_BLOCKS, RG_BLOCK_W, RG_BLOCK_W), f32) * RG_BLOCK_W ** -0.5
    b_rg_x = 0.01 * jax.random.normal(ks[8], (DEPTH, D_RNN), f32)
    a0 = jax.random.uniform(ks[9], (DEPTH, D_RNN), f32, minval=0.9, maxval=0.999)
    lru_lambda = jnp.log(a0) - jnp.log1p(-a0)
    w_out = jax.random.normal(ks[10], (DEPTH, D_MODEL, D_MODEL), f32) * D_MODEL ** -0.5
    norm2_g = 1.0 + 0.01 * jax.random.normal(ks[11], (DEPTH, D_MODEL), f32)
    w_mlp1 = jax.random.normal(ks[12], (DEPTH, D_MODEL, D_FF), f32) * D_MODEL ** -0.5
    w_mlp2 = jax.random.normal(ks[13], (DEPTH, D_FF, D_MODEL), f32) * D_FF ** -0.5
    rel_bias = 0.1 * jax.random.normal(ks[14], (N_BUCKETS, N_HEADS), f32)
    meta_tokens = jax.random.normal(ks[15], (N_META, D_MODEL), f32)
    final_g = 1.0 + 0.01 * jax.random.normal(ks[16], (D_MODEL,), f32)
    return {"x": x, "norm1_g": norm1_g, "w_in": w_in, "conv_w": conv_w, "conv_b": conv_b,
            "w_rg_a": w_rg_a, "b_rg_a": b_rg_a, "w_rg_x": w_rg_x, "b_rg_x": b_rg_x,
            "lru_lambda": lru_lambda, "w_out": w_out, "norm2_g": norm2_g,
            "w_mlp1": w_mlp1, "w_mlp2": w_mlp2, "rel_bias": rel_bias,
            "meta_tokens": meta_tokens, "final_g": final_g}


def reference(x, norm1_g, w_in, conv_w, conv_b, w_rg_a, b_rg_a, w_rg_x, b_rg_x,
              lru_lambda, w_out, norm2_g, w_mlp1, w_mlp2, rel_bias, meta_tokens, final_g):
    B = x.shape[0]
    meta = jnp.broadcast_to(meta_tokens[None].astype(x.dtype), (B, N_META, D_MODEL))
    h = jnp.concatenate([meta, x], axis=1)
    for l in range(DEPTH):
        h = hybrid_layer(h, norm1_g[l], w_in[l], conv_w[l], conv_b[l], w_rg_a[l], b_rg_a[l],
                         w_rg_x[l], b_rg_x[l], lru_lambda[l], w_out[l], norm2_g[l],
                         w_mlp1[l], w_mlp2[l], rel_bias)
    h = rms_norm(h, final_g)
    return h[:, N_META:]
```

```python
import functools
import math

import numpy as np
import jax
import jax.numpy as jnp
from jax import lax
from jax.experimental import pallas as pl
from jax.experimental.pallas import tpu as pltpu

D_MODEL = 1024
N_META = 16
D_RNN = D_MODEL
RG_BLOCKS = 4
RG_BLOCK_W = D_RNN // RG_BLOCKS
CONV_W = 4
LRU_C = 8.0
N_HEADS = 8
HEAD_DIM = D_MODEL // N_HEADS
IDX_HEADS = 8
IDX_DIM = 64
TOP_K_MAX = 256
N_BUCKETS = 32
MAX_DISTANCE = 128
D_FF = 4 * D_MODEL
NORM_EPS = 1e-6
IN_SIZES = (D_RNN, D_RNN, D_MODEL, D_MODEL, D_MODEL, IDX_HEADS * IDX_DIM, IDX_DIM, IDX_HEADS, D_MODEL, D_MODEL)

LANES = 128
SUBLANES = 8
VMEM_LIMIT_BYTES = 56 * 1024 * 1024

SEQ_TILE = 256
IDX_PAD = LANES
WI_ROWS = 16
COUNT_ROWS = 64

MXU_DTYPE = jnp.bfloat16
F32 = jnp.float32
INT_MIN = -(2 ** 31)


def _rms(x, g):
    ms = jnp.mean(x * x, axis=-1, keepdims=True)
    return x * lax.rsqrt(ms + NORM_EPS) * g


N_NAT = 4 * D_MODEL + IDX_PAD
N_TR = 3 * D_MODEL + IDX_HEADS * IDX_PAD + WI_ROWS


def _proj_kernel(h_ref, g_ref, wn_ref, wt_ref,
                 ug_ref, ur_ref, gr_ref, k_ref, ki_ref,
                 qT_ref, vT_ref, gaT_ref, qiT_ref, wiT_ref):
    xn = _rms(h_ref[...], g_ref[...]).astype(MXU_DTYPE)

    def nat(c0, n):
        return jnp.dot(xn, wn_ref[:, c0:c0 + n], preferred_element_type=F32)

    def tr(r0, n):
        return lax.dot_general(wt_ref[r0:r0 + n, :], xn, (((1,), (1,)), ((), ())),
                               preferred_element_type=F32)

    ug_ref[...] = nat(0, D_MODEL)
    ur_ref[...] = nat(D_MODEL, D_MODEL)
    gr_ref[...] = nat(2 * D_MODEL, D_MODEL)
    k_ref[...] = nat(3 * D_MODEL, D_MODEL).astype(k_ref.dtype)
    ki_ref[...] = nat(4 * D_MODEL, IDX_PAD).astype(ki_ref.dtype)
    qT_ref[...] = tr(0, D_MODEL).astype(qT_ref.dtype)
    vT_ref[...] = tr(D_MODEL, D_MODEL).astype(vT_ref.dtype)
    gaT_ref[...] = tr(2 * D_MODEL, D_MODEL)
    qiT_ref[...] = tr(3 * D_MODEL, IDX_HEADS * IDX_PAD).astype(qiT_ref.dtype)
    wiT_ref[...] = tr(3 * D_MODEL + IDX_HEADS * IDX_PAD, WI_ROWS)


def _proj(h, g1, wn, wt):
    B, Lp, D = h.shape
    T = SEQ_TILE
    nat_spec = lambda n: pl.BlockSpec((None, T, n), lambda b, i: (b, i, 0))
    tr_spec = lambda n: pl.BlockSpec((None, n, T), lambda b, i: (b, 0, i))
    const = lambda shape: pl.BlockSpec(shape, lambda b, i: (0,) * len(shape), pipeline_mode=pl.Buffered(1))
    sds = jax.ShapeDtypeStruct
    return pl.pallas_call(
        _proj_kernel,
        grid=(B, Lp // T),
        in_specs=[nat_spec(D), const((1, D)), const((D, N_NAT)), const((N_TR, D))],
        out_specs=[nat_spec(D), nat_spec(D), nat_spec(D), nat_spec(D), nat_spec(IDX_PAD),
                   tr_spec(D), tr_spec(D), tr_spec(D), tr_spec(IDX_HEADS * IDX_PAD), tr_spec(WI_ROWS)],
        out_shape=[sds((B, Lp, D), F32), sds((B, Lp, D), F32), sds((B, Lp, D), F32),
                   sds((B, Lp, D), MXU_DTYPE), sds((B, Lp, IDX_PAD), MXU_DTYPE),
                   sds((B, D, Lp), MXU_DTYPE), sds((B, D, Lp), MXU_DTYPE), sds((B, D, Lp), F32),
                   sds((B, IDX_HEADS * IDX_PAD, Lp), MXU_DTYPE), sds((B, WI_ROWS, Lp), F32)],
        compiler_params=pltpu.CompilerParams(dimension_semantics=("arbitrary", "arbitrary"),
                                             vmem_limit_bytes=VMEM_LIMIT_BYTES),
        name="proj",
    )(h, g1, wn, wt)


def _rglru_kernel(ur_ref, ug_ref, gr_ref, cw_ref, cb_ref, wa_ref, ba_ref, wx_ref, bx_ref, lam_ref,
                  y_ref, ext_ref, hst_ref):
    T = SEQ_TILE

    @pl.when(pl.program_id(1) == 0)
    def _():
        ext_ref[0:SUBLANES, :] = jnp.zeros((SUBLANES, D_RNN), F32)
        hst_ref[...] = jnp.zeros_like(hst_ref)

    x = ur_ref[...]
    ext_ref[SUBLANES:SUBLANES + T, :] = x
    base = SUBLANES - (CONV_W - 1)
    xc = cw_ref[0:1, :] * ext_ref[base:base + T, :]
    for j in range(1, CONV_W):
        xc = xc + cw_ref[j:j + 1, :] * ext_ref[base + j:base + j + T, :]
    xc = xc + cb_ref[...]
    ext_ref[0:SUBLANES, :] = x[T - SUBLANES:T, :]

    xcb = xc.astype(MXU_DTYPE)

    def block_diag(w_ref, b_ref):
        parts = [jnp.dot(xcb[:, g * RG_BLOCK_W:(g + 1) * RG_BLOCK_W], w_ref[g], preferred_element_type=F32)
                 for g in range(RG_BLOCKS)]
        return jnp.concatenate(parts, axis=-1) + b_ref[...]

    r = jax.nn.sigmoid(block_diag(wa_ref, ba_ref))
    gate_i = jax.nn.sigmoid(block_diag(wx_ref, bx_ref))
    z = -lam_ref[...]
    softplus = jnp.maximum(z, 0.0) + jnp.log1p(jnp.exp(-jnp.abs(z)))
    log_a = (-LRU_C) * r * softplus
    a = jnp.exp(log_a)
    u = jnp.sqrt(-jnp.tanh(log_a) * (a * a + 1.0)) * (gate_i * xc)

    row = lax.broadcasted_iota(jnp.int32, (T, D_RNN), 0)
    s = 1
    while s < T:
        a_prev = pltpu.roll(a, s, axis=0)
        u_prev = pltpu.roll(u, s, axis=0)
        valid = row >= s
        u = jnp.where(valid, a * u_prev + u, u)
        a = jnp.where(valid, a * a_prev, a)
        s *= 2
    h = a * hst_ref[...] + u
    hst_ref[...] = h[T - 1:T, :]
    y_ref[...] = h * jax.nn.gelu(ug_ref[...]) * jax.nn.sigmoid(gr_ref[...])


def _rglru(ur, ug, gr, cw, cb, wa, ba, wx, bx, lam):
    B, Lp, D = ur.shape
    T = SEQ_TILE
    row_spec = pl.BlockSpec((None, T, D), lambda b, i: (b, i, 0))
    const = lambda shape: pl.BlockSpec(shape, lambda b, i: (0,) * len(shape))
    return pl.pallas_call(
        _rglru_kernel,
        grid=(B, Lp // T),
        in_specs=[row_spec, row_spec, row_spec, const((CONV_W, D)), const((1, D)),
                  const((RG_BLOCKS, RG_BLOCK_W, RG_BLOCK_W)), const((1, D)),
                  const((RG_BLOCKS, RG_BLOCK_W, RG_BLOCK_W)), const((1, D)), const((1, D))],
        out_specs=row_spec,
        out_shape=jax.ShapeDtypeStruct((B, Lp, D), F32),
        scratch_shapes=[pltpu.VMEM((SUBLANES + T, D), F32), pltpu.VMEM((1, D), F32)],
        compiler_params=pltpu.CompilerParams(dimension_semantics=("arbitrary", "arbitrary"),
                                             vmem_limit_bytes=VMEM_LIMIT_BYTES),
        name="rglru",
    )(ur, ug, gr, cw, cb, wa, ba, wx, bx, lam)


def _to_sort_key(s):
    bits = lax.bitcast_convert_type(s, jnp.int32)
    return bits ^ ((bits >> 31) & jnp.int32(0x7FFFFFFF))


def _attn_kernel(top_k, qiT_ref, wiT_ref, ki_ref, qT_ref, gaT_ref, tz_ref, kn_hbm, vT_hbm,
                 yT_ref,
                 keys_ref, kbuf, vbuf, sem, m_ref, l_ref, acc_ref, madd_ref):
    T = SEQ_TILE
    b = pl.program_id(0)
    qb = pl.program_id(1)
    nblk = qb + 1

    def k_copy(kb, slot):
        return pltpu.make_async_copy(kn_hbm.at[b, pl.ds(kb * T, T), :], kbuf.at[slot], sem.at[0, slot])

    def v_copy(kb, slot):
        return pltpu.make_async_copy(vT_hbm.at[b, :, pl.ds(kb * T, T)], vbuf.at[slot], sem.at[1, slot])

    k_copy(0, 0).start()
    v_copy(0, 0).start()

    def score_tile(kb):
        kib = ki_ref[pl.ds(pl.multiple_of(kb * T, T), T), :]
        acc = jnp.zeros((T, T), F32)
        for h in range(IDX_HEADS):
            s = jnp.dot(kib, qiT_ref[h * IDX_PAD:(h + 1) * IDX_PAD, :], preferred_element_type=F32)
            acc = acc + jnp.maximum(s, 0.0) * wiT_ref[h:h + 1, :]
        return acc

    def far_scores(kb, c):
        keys_ref[pl.ds(pl.multiple_of(kb * T, T), T), :] = _to_sort_key(score_tile(kb))
        return c

    lax.fori_loop(0, qb, far_scores, 0)
    kpos = lax.broadcasted_iota(jnp.int32, (T, T), 0)
    qpos = lax.broadcasted_iota(jnp.int32, (T, T), 1)
    causal = kpos <= qpos
    keys_ref[pl.ds(pl.multiple_of(qb * T, T), T), :] = jnp.where(causal, _to_sort_key(score_tile(qb)),
                                                                 jnp.int32(INT_MIN))

    n_chunks = nblk * (T // COUNT_ROWS)

    def count_ge(trial):
        trial8 = jnp.broadcast_to(trial, (SUBLANES, T))

        def body(r, acc):
            r0 = pl.multiple_of(r * COUNT_ROWS, COUNT_ROWS)
            for j in range(COUNT_ROWS // SUBLANES):
                blk = keys_ref[pl.ds(r0 + j * SUBLANES, SUBLANES), :]
                acc = acc + (blk >= trial8).astype(jnp.int32)
            return acc

        acc = lax.fori_loop(0, n_chunks, body, jnp.zeros((SUBLANES, T), jnp.int32))
        return jnp.sum(acc, axis=0, keepdims=True)

    zero = jnp.zeros((1, T), jnp.int32)
    lo = jnp.where(count_ge(zero) >= top_k, zero, jnp.int32(INT_MIN))

    def bit_step(i, lo):
        trial = lo + lax.shift_left(jnp.int32(1), jnp.int32(30) - i)
        return jnp.where(count_ge(trial) >= top_k, trial, lo)

    lo = lax.fori_loop(0, 31, bit_step, lo)
    tau = jnp.maximum(lo, jnp.int32(INT_MIN + 1))

    m_ref[...] = jnp.full(m_ref.shape, -jnp.inf, F32)
    l_ref[...] = jnp.zeros_like(l_ref)
    acc_ref[...] = jnp.zeros_like(acc_ref)

    def process(kb, kind):
        slot = lax.rem(kb, 2)
        k_copy(kb, slot).wait()
        v_copy(kb, slot).wait()

        @pl.when(kb + 1 < nblk)
        def _():
            k_copy(kb + 1, 1 - slot).start()
            v_copy(kb + 1, 1 - slot).start()

        kblk = keys_ref[pl.ds(pl.multiple_of(kb * T, T), T), :]
        madd_ref[...] = jnp.where(kblk >= tau, 0.0, -jnp.inf).astype(F32)
        for h in range(N_HEADS):
            hs = slice(h * HEAD_DIM, (h + 1) * HEAD_DIM)
            s = jnp.dot(kbuf[slot, :, hs], qT_ref[hs, :], preferred_element_type=F32)
            s = s + madd_ref[...]
            if kind == 1:
                s = s + tz_ref[1, h]
            elif kind == 2:
                s = s + tz_ref[0, h]
            m_old = m_ref[h:h + 1, :]
            m_new = jnp.maximum(m_old, jnp.max(s, axis=0, keepdims=True))
            m_safe = jnp.where(m_new == -jnp.inf, 0.0, m_new)
            p = jnp.exp(s - m_safe)
            alpha = jnp.exp(m_old - m_safe)
            l_ref[h:h + 1, :] = alpha * l_ref[h:h + 1, :] + jnp.sum(p, axis=0, keepdims=True)
            pv = jnp.dot(vbuf[slot, hs, :], p.astype(MXU_DTYPE), preferred_element_type=F32)
            acc_ref[hs, :] = alpha * acc_ref[hs, :] + pv
            m_ref[h:h + 1, :] = m_new

    def far_block(kb, c):
        process(kb, 0)
        return c

    lax.fori_loop(0, jnp.maximum(qb - 1, 0), far_block, 0)

    @pl.when(qb >= 1)
    def _():
        process(qb - 1, 1)

    process(qb, 2)

    for h in range(N_HEADS):
        hs = slice(h * HEAD_DIM, (h + 1) * HEAD_DIM)
        out = acc_ref[hs, :] / l_ref[h:h + 1, :]
        yT_ref[hs, :] = out * jax.nn.sigmoid(gaT_ref[hs, :])


def _attention(qiT, wiT, ki, qT, gaT, tz, kn, vT, top_k):
    B, D, Lp = qT.shape
    T = SEQ_TILE
    tr_spec = lambda n: pl.BlockSpec((None, n, T), lambda b, i: (b, 0, i))
    return pl.pallas_call(
        functools.partial(_attn_kernel, top_k),
        grid=(B, Lp // T),
        in_specs=[tr_spec(IDX_HEADS * IDX_PAD), tr_spec(WI_ROWS),
                  pl.BlockSpec((None, Lp, IDX_PAD), lambda b, i: (b, 0, 0), pipeline_mode=pl.Buffered(1)),
                  tr_spec(D), tr_spec(D),
                  pl.BlockSpec((2, N_HEADS, T, T), lambda b, i: (0, 0, 0, 0), pipeline_mode=pl.Buffered(1)),
                  pl.BlockSpec(memory_space=pl.ANY), pl.BlockSpec(memory_space=pl.ANY)],
        out_specs=tr_spec(D),
        out_shape=jax.ShapeDtypeStruct((B, D, Lp), F32),
        scratch_shapes=[pltpu.VMEM((Lp, T), jnp.int32),
                        pltpu.VMEM((2, T, D), MXU_DTYPE), pltpu.VMEM((2, D, T), MXU_DTYPE),
                        pltpu.SemaphoreType.DMA((2, 2)),
                        pltpu.VMEM((N_HEADS, T), F32), pltpu.VMEM((N_HEADS, T), F32),
                        pltpu.VMEM((D, T), F32), pltpu.VMEM((T, T), F32)],
        compiler_params=pltpu.CompilerParams(dimension_semantics=("arbitrary", "arbitrary"),
                                             vmem_limit_bytes=VMEM_LIMIT_BYTES),
        name="attn",
    )(qiT, wiT, ki, qT, gaT, tz, kn, vT)


def _out_kernel(last, h_ref, yr_ref, yT_ref, wo_ref, g2_ref, w1_ref, w2_ref, fg_ref, o_ref):
    y = yr_ref[...] + yT_ref[...].T
    h1 = h_ref[...] + jnp.dot(y.astype(MXU_DTYPE), wo_ref[...], preferred_element_type=F32)
    hn = _rms(h1, g2_ref[...]).astype(MXU_DTYPE)
    a = jnp.dot(hn, w1_ref[...], preferred_element_type=F32)
    a = jnp.square(jnp.maximum(a, 0.0)).astype(MXU_DTYPE)
    h2 = h1 + jnp.dot(a, w2_ref[...], preferred_element_type=F32)
    if last:
        h2 = _rms(h2, fg_ref[...])
    o_ref[...] = h2


def _out_mlp(h, yr, yT, wo, g2, w1, w2, fg, last):
    B, Lp, D = h.shape
    T = SEQ_TILE
    row_spec = pl.BlockSpec((None, T, D), lambda b, i: (b, i, 0))
    const = lambda shape: pl.BlockSpec(shape, lambda b, i: (0,) * len(shape), pipeline_mode=pl.Buffered(1))
    return pl.pallas_call(
        functools.partial(_out_kernel, last),
        grid=(B, Lp // T),
        in_specs=[row_spec, row_spec, pl.BlockSpec((None, D, T), lambda b, i: (b, 0, i)),
                  const((D, D)), const((1, D)), const((D, D_FF)), const((D_FF, D)), const((1, D))],
        out_specs=row_spec,
        out_shape=jax.ShapeDtypeStruct((B, Lp, D), F32),
        compiler_params=pltpu.CompilerParams(dimension_semantics=("arbitrary", "arbitrary"),
                                             vmem_limit_bytes=VMEM_LIMIT_BYTES),
        name="out_mlp",
    )(h, yr, yT, wo, g2, w1, w2, fg)


def _t5_bucket_table(n):
    max_exact = N_BUCKETS // 2
    d = np.arange(n)
    df = np.maximum(d, 1).astype(np.float64)
    large = max_exact + (np.log(df / max_exact) / math.log(MAX_DISTANCE / max_exact)
                         * (N_BUCKETS - max_exact)).astype(np.int32)
    large = np.minimum(large, N_BUCKETS - 1)
    return np.where(d < max_exact, d, large).astype(np.int32)


def _bias_tiles(rel_bias):
    T = SEQ_TILE
    buckets = _t5_bucket_table(2 * T)
    assert np.all(buckets[T + 1:] == N_BUCKETS - 1)
    table = rel_bias[buckets] - rel_bias[N_BUCKETS - 1][None, :]
    k = np.arange(T)[:, None]
    q = np.arange(T)[None, :]
    dist = np.stack([np.clip(q - k, 0, 2 * T - 1), T + q - k])
    return jnp.transpose(table[dist], (0, 3, 1, 2)).astype(F32)


def _layer_weights(w_in):
    offs = np.concatenate([[0], np.cumsum(IN_SIZES)])
    parts = [w_in[:, int(offs[i]):int(offs[i + 1])] for i in range(len(IN_SIZES))]
    w_ug, w_ur, w_q, w_k, w_v, w_qi, w_ki, w_wi, w_gr, w_ga = parts
    D = w_in.shape[0]
    w_ki = jnp.pad(w_ki, ((0, 0), (0, IDX_PAD - IDX_DIM)))
    wn = jnp.concatenate([w_ug, w_ur, w_gr, w_k, w_ki], axis=1).astype(MXU_DTYPE)
    w_qi = jnp.pad(w_qi.reshape(D, IDX_HEADS, IDX_DIM), ((0, 0), (0, 0), (0, IDX_PAD - IDX_DIM)))
    w_qi = w_qi.reshape(D, IDX_HEADS * IDX_PAD)
    w_wi = jnp.pad(w_wi, ((0, 0), (0, WI_ROWS - IDX_HEADS)))
    scale = HEAD_DIM ** -0.5
    wt = jnp.concatenate([w_q * scale, w_v, w_ga, w_qi, w_wi], axis=1).T.astype(MXU_DTYPE)
    return wn, wt


def kernel(x, norm1_g, w_in, conv_w, conv_b, w_rg_a, b_rg_a, w_rg_x, b_rg_x, lru_lambda, w_out, norm2_g,
           w_mlp1, w_mlp2, rel_bias, meta_tokens, final_g):
    B, S, D = x.shape
    depth = w_in.shape[0]
    L = S + N_META
    T = SEQ_TILE
    Lp = -(-L // T) * T
    top_k = min(TOP_K_MAX, L // 4)
    meta = jnp.broadcast_to(meta_tokens[None].astype(x.dtype), (B, N_META, D))
    h = jnp.concatenate([meta, x, jnp.zeros((B, Lp - L, D), x.dtype)], axis=1)
    tz = _bias_tiles(rel_bias)
    row = lambda v: v.reshape(1, -1)
    for l in range(depth):
        wn, wt = _layer_weights(w_in[l])
        ug, ur, gr, kn, ki, qT, vT, gaT, qiT, wiT = _proj(h, row(norm1_g[l]), wn, wt)
        yr = _rglru(ur, ug, gr, conv_w[l], row(conv_b[l]), w_rg_a[l].astype(MXU_DTYPE), row(b_rg_a[l]),
                    w_rg_x[l].astype(MXU_DTYPE), row(b_rg_x[l]), row(lru_lambda[l]))
        yT = _attention(qiT, wiT, ki, qT, gaT, tz, kn, vT, top_k)
        h = _out_mlp(h, yr, yT, w_out[l].astype(MXU_DTYPE), row(norm2_g[l]), w_mlp1[l].astype(MXU_DTYPE),
                     w_mlp2[l].astype(MXU_DTYPE), row(final_g), last=(l == depth - 1))
    return h[:, N_META:L]
```

```python
import functools
import math

import numpy as np
import jax
import jax.numpy as jnp
from jax import lax
from jax.experimental import pallas as pl
from jax.experimental.pallas import tpu as pltpu

D_MODEL = 1024
N_META = 16
D_RNN = D_MODEL
RG_BLOCKS = 4
RG_BLOCK_W = D_RNN // RG_BLOCKS
CONV_W = 4
LRU_C = 8.0
N_HEADS = 8
HEAD_DIM = D_MODEL // N_HEADS
IDX_HEADS = 8
IDX_DIM = 64
TOP_K_MAX = 256
N_BUCKETS = 32
MAX_DISTANCE = 128
D_FF = 4 * D_MODEL
NORM_EPS = 1e-6
IN_SIZES = (D_RNN, D_RNN, D_MODEL, D_MODEL, D_MODEL, IDX_HEADS * IDX_DIM, IDX_DIM, IDX_HEADS, D_MODEL, D_MODEL)

LANES = 128
SUBLANES = 8
VMEM_LIMIT_BYTES = 56 * 1024 * 1024

SEQ_TILE = 256
IDX_PAD = LANES
WI_ROWS = 16
COUNT_ROWS = 32
KV_SLOTS = 3

MXU_DTYPE = jnp.bfloat16
F32 = jnp.float32
INT_MIN = -(2 ** 31)
INT_MAX = 2 ** 31 - 1
LOG2E = math.log2(math.e)


def _rms(x, g):
    ms = jnp.mean(x * x, axis=-1, keepdims=True)
    return x * lax.rsqrt(ms + NORM_EPS) * g


N_NAT = 4 * D_MODEL + IDX_PAD
N_TR = 3 * D_MODEL + IDX_HEADS * IDX_PAD + WI_ROWS


def _proj_kernel(h_ref, g_ref, wn_ref, wt_ref,
                 ug_ref, ur_ref, gr_ref, k_ref, ki_ref,
                 qT_ref, vT_ref, gaT_ref, qiT_ref, wiT_ref):
    xn = _rms(h_ref[...], g_ref[...]).astype(MXU_DTYPE)

    def nat(c0, n):
        return jnp.dot(xn, wn_ref[:, c0:c0 + n], preferred_element_type=F32)

    def tr(r0, n):
        return lax.dot_general(wt_ref[r0:r0 + n, :], xn, (((1,), (1,)), ((), ())),
                               preferred_element_type=F32)

    ug_ref[...] = nat(0, D_MODEL)
    ur_ref[...] = nat(D_MODEL, D_MODEL)
    gr_ref[...] = nat(2 * D_MODEL, D_MODEL)
    k_ref[...] = nat(3 * D_MODEL, D_MODEL).astype(k_ref.dtype)
    ki_ref[...] = nat(4 * D_MODEL, IDX_PAD).astype(ki_ref.dtype)
    qT_ref[...] = tr(0, D_MODEL).astype(qT_ref.dtype)
    vT_ref[...] = tr(D_MODEL, D_MODEL).astype(vT_ref.dtype)
    gaT_ref[...] = tr(2 * D_MODEL, D_MODEL)
    qiT_ref[...] = tr(3 * D_MODEL, IDX_HEADS * IDX_PAD).astype(qiT_ref.dtype)
    wiT_ref[...] = tr(3 * D_MODEL + IDX_HEADS * IDX_PAD, WI_ROWS)


def _proj(h, g1, wn, wt):
    B, Lp, D = h.shape
    T = SEQ_TILE
    nat_spec = lambda n: pl.BlockSpec((None, T, n), lambda b, i: (b, i, 0))
    tr_spec = lambda n: pl.BlockSpec((None, n, T), lambda b, i: (b, 0, i))
    const = lambda shape: pl.BlockSpec(shape, lambda b, i: (0,) * len(shape), pipeline_mode=pl.Buffered(1))
    sds = jax.ShapeDtypeStruct
    return pl.pallas_call(
        _proj_kernel,
        grid=(B, Lp // T),
        in_specs=[nat_spec(D), const((1, D)), const((D, N_NAT)), const((N_TR, D))],
        out_specs=[nat_spec(D), nat_spec(D), nat_spec(D), nat_spec(D), nat_spec(IDX_PAD),
                   tr_spec(D), tr_spec(D), tr_spec(D), tr_spec(IDX_HEADS * IDX_PAD), tr_spec(WI_ROWS)],
        out_shape=[sds((B, Lp, D), F32), sds((B, Lp, D), F32), sds((B, Lp, D), F32),
                   sds((B, Lp, D), MXU_DTYPE), sds((B, Lp, IDX_PAD), MXU_DTYPE),
                   sds((B, D, Lp), MXU_DTYPE), sds((B, D, Lp), MXU_DTYPE), sds((B, D, Lp), F32),
                   sds((B, IDX_HEADS * IDX_PAD, Lp), MXU_DTYPE), sds((B, WI_ROWS, Lp), F32)],
        compiler_params=pltpu.CompilerParams(dimension_semantics=("arbitrary", "arbitrary"),
                                             vmem_limit_bytes=VMEM_LIMIT_BYTES),
        name="proj",
    )(h, g1, wn, wt)


def _rglru_kernel(ur_ref, ug_ref, gr_ref, cw_ref, cb_ref, wa_ref, ba_ref, wx_ref, bx_ref, lam_ref,
                  y_ref, ext_ref, hst_ref):
    T = SEQ_TILE

    @pl.when(pl.program_id(1) == 0)
    def _():
        ext_ref[0:SUBLANES, :] = jnp.zeros((SUBLANES, D_RNN), F32)
        hst_ref[...] = jnp.zeros_like(hst_ref)

    x = ur_ref[...]
    ext_ref[SUBLANES:SUBLANES + T, :] = x
    base = SUBLANES - (CONV_W - 1)
    xc = cw_ref[0:1, :] * ext_ref[base:base + T, :]
    for j in range(1, CONV_W):
        xc = xc + cw_ref[j:j + 1, :] * ext_ref[base + j:base + j + T, :]
    xc = xc + cb_ref[...]
    ext_ref[0:SUBLANES, :] = x[T - SUBLANES:T, :]

    xcb = xc.astype(MXU_DTYPE)

    def block_diag(w_ref, b_ref):
        parts = [jnp.dot(xcb[:, g * RG_BLOCK_W:(g + 1) * RG_BLOCK_W], w_ref[g], preferred_element_type=F32)
                 for g in range(RG_BLOCKS)]
        return jnp.concatenate(parts, axis=-1) + b_ref[...]

    r = jax.nn.sigmoid(block_diag(wa_ref, ba_ref))
    gate_i = jax.nn.sigmoid(block_diag(wx_ref, bx_ref))
    z = -lam_ref[...]
    softplus = jnp.maximum(z, 0.0) + jnp.log1p(jnp.exp(-jnp.abs(z)))
    log_a = (-LRU_C) * r * softplus
    a = jnp.exp(log_a)
    u = jnp.sqrt(-jnp.tanh(log_a) * (a * a + 1.0)) * (gate_i * xc)

    row = lax.broadcasted_iota(jnp.int32, (T, D_RNN), 0)
    s = 1
    while s < T:
        a_prev = pltpu.roll(a, s, axis=0)
        u_prev = pltpu.roll(u, s, axis=0)
        valid = row >= s
        u = jnp.where(valid, a * u_prev + u, u)
        a = jnp.where(valid, a * a_prev, a)
        s *= 2
    h = a * hst_ref[...] + u
    hst_ref[...] = h[T - 1:T, :]
    y_ref[...] = h * jax.nn.gelu(ug_ref[...]) * jax.nn.sigmoid(gr_ref[...])


def _rglru(ur, ug, gr, cw, cb, wa, ba, wx, bx, lam):
    B, Lp, D = ur.shape
    T = SEQ_TILE
    row_spec = pl.BlockSpec((None, T, D), lambda b, i: (b, i, 0))
    const = lambda shape: pl.BlockSpec(shape, lambda b, i: (0,) * len(shape))
    return pl.pallas_call(
        _rglru_kernel,
        grid=(B, Lp // T),
        in_specs=[row_spec, row_spec, row_spec, const((CONV_W, D)), const((1, D)),
                  const((RG_BLOCKS, RG_BLOCK_W, RG_BLOCK_W)), const((1, D)),
                  const((RG_BLOCKS, RG_BLOCK_W, RG_BLOCK_W)), const((1, D)), const((1, D))],
        out_specs=row_spec,
        out_shape=jax.ShapeDtypeStruct((B, Lp, D), F32),
        scratch_shapes=[pltpu.VMEM((SUBLANES + T, D), F32), pltpu.VMEM((1, D), F32)],
        compiler_params=pltpu.CompilerParams(dimension_semantics=("arbitrary", "arbitrary"),
                                             vmem_limit_bytes=VMEM_LIMIT_BYTES),
        name="rglru",
    )(ur, ug, gr, cw, cb, wa, ba, wx, bx, lam)


def _to_sort_key(s):
    bits = lax.bitcast_convert_type(s, jnp.int32)
    return bits ^ ((bits >> 31) & jnp.int32(0x7FFFFFFF))


def _attn_kernel(top_k, qiT_ref, wiT_ref, ki_ref, qT_ref, gaT_ref, tab_ref, kn_hbm, vT_hbm,
                 yT_ref,
                 keys_ref, kbuf, vbuf, sem, m_ref, l_ref, alpha_ref, acc_ref, s_ref, p_ref, tz_ref):
    T = SEQ_TILE
    b = pl.program_id(0)
    qb = pl.program_id(1)
    nblk = qb + 1

    n_pairs = jnp.maximum((nblk - 1) // 2, 1)
    npos = 2 + 2 * n_pairs

    def tile_of(j):
        return jnp.maximum(nblk - 1 - j, 0)

    def k_copy(j):
        slot = lax.rem(j, KV_SLOTS)
        return pltpu.make_async_copy(kn_hbm.at[b, pl.ds(tile_of(j) * T, T), :], kbuf.at[slot], sem.at[0, slot])

    def v_copy(j):
        slot = lax.rem(j, KV_SLOTS)
        return pltpu.make_async_copy(vT_hbm.at[b, :, pl.ds(tile_of(j) * T, T)], vbuf.at[slot], sem.at[1, slot])

    v_copy(0).start()
    for j in range(KV_SLOTS):
        k_copy(j).start()

    @pl.when((b == 0) & (qb == 0))
    def _():
        for h in range(N_HEADS):
            rows = jnp.broadcast_to(tab_ref[h:h + 1, :], (T, 2 * T))
            skew = pltpu.roll(rows, 0, axis=1, stride=1, stride_axis=0)
            tz_ref[0, h] = skew[:, 0:T]
            tz_ref[1, h] = skew[:, T:2 * T]

    def score_tile(kb):
        kib = ki_ref[pl.ds(pl.multiple_of(kb * T, T), T), :]
        acc = jnp.zeros((T, T), F32)
        for h in range(IDX_HEADS):
            s = jnp.dot(kib, qiT_ref[h * IDX_PAD:(h + 1) * IDX_PAD, :], preferred_element_type=F32)
            acc = acc + jnp.maximum(s, 0.0) * wiT_ref[h:h + 1, :]
        return acc

    def far_scores(kb, c):
        keys_ref[pl.ds(pl.multiple_of(kb * T, T), T), :] = _to_sort_key(score_tile(kb))
        return c

    lax.fori_loop(0, qb, far_scores, 0)
    kpos = lax.broadcasted_iota(jnp.int32, (T, T), 0)
    qpos = lax.broadcasted_iota(jnp.int32, (T, T), 1)
    keys_ref[pl.ds(pl.multiple_of(qb * T, T), T), :] = jnp.where(kpos <= qpos, _to_sort_key(score_tile(qb)),
                                                                 jnp.int32(INT_MIN))

    def count_rows(pred):
        def body(kb, acc):
            base = pl.multiple_of(kb * T, T)
            for j in range(T // COUNT_ROWS):
                blk = keys_ref[pl.ds(base + j * COUNT_ROWS, COUNT_ROWS), :]
                acc = acc + pred(blk, base + j * COUNT_ROWS).astype(jnp.int32)
            return acc

        acc = lax.fori_loop(0, nblk, body, jnp.zeros((COUNT_ROWS, T), jnp.int32))
        return jnp.sum(acc, axis=0, keepdims=True)

    def count_ge(trial):
        trial_b = jnp.broadcast_to(trial, (COUNT_ROWS, T))
        return count_rows(lambda blk, r0: blk >= trial_b)

    zero = jnp.zeros((1, T), jnp.int32)
    lo = jnp.where(count_ge(zero) >= top_k, zero, jnp.int32(INT_MIN))

    def bit_step(i, lo):
        trial = lo + lax.shift_left(jnp.int32(1), jnp.int32(30) - i)
        return jnp.where(count_ge(trial) >= top_k, trial, lo)

    lo = lax.fori_loop(0, 31, bit_step, lo)
    tau = jnp.maximum(lo, jnp.int32(INT_MIN + 1))

    n_ge = count_ge(tau)
    n_gt = jnp.where(tau == INT_MAX, 0, count_ge(tau + 1))
    need = top_k - n_gt
    excess = (n_ge - n_gt) > need

    @pl.when(jnp.max(excess.astype(jnp.int32)) > 0)
    def _():
        row_iota = lax.broadcasted_iota(jnp.int32, (COUNT_ROWS, T), 0)
        tau_b = jnp.broadcast_to(tau, (COUNT_ROWS, T))

        def count_ties_upto(cut):
            cut_b = jnp.broadcast_to(cut, (COUNT_ROWS, T))
            return count_rows(lambda blk, r0: (blk == tau_b) & (row_iota + r0 <= cut_b))

        n_bits = max(1, int(keys_ref.shape[0] - 1).bit_length())

        def idx_step(i, cut):
            trial = cut - lax.shift_left(jnp.int32(1), jnp.int32(n_bits - 1) - i)
            return jnp.where(count_ties_upto(trial) >= need, trial, cut)

        cut = lax.fori_loop(0, n_bits, idx_step, jnp.full((1, T), 2 ** n_bits - 1, jnp.int32))
        cut = jnp.where(excess, cut, jnp.int32(INT_MAX))
        cut_b = jnp.broadcast_to(cut, (COUNT_ROWS, T))

        def demote(kb, c):
            base = pl.multiple_of(kb * T, T)
            for j in range(T // COUNT_ROWS):
                rows = pl.ds(base + j * COUNT_ROWS, COUNT_ROWS)
                blk = keys_ref[rows, :]
                drop = (blk == tau_b) & (row_iota + (base + j * COUNT_ROWS) > cut_b)
                keys_ref[rows, :] = jnp.where(drop, tau_b - 1, blk)
            return c

        lax.fori_loop(0, nblk, demote, 0)

    m_ref[...] = jnp.full(m_ref.shape, -jnp.inf, F32)
    l_ref[...] = jnp.zeros_like(l_ref)
    acc_ref[...] = jnp.zeros_like(acc_ref)

    def logits(j, slot):
        kslot = lax.rem(j, KV_SLOTS)
        tau_j = jnp.where(j < nblk, tau, jnp.int32(INT_MAX))
        kblk = keys_ref[pl.ds(pl.multiple_of(tile_of(j) * T, T), T), :]
        madd = jnp.where(kblk >= tau_j, 0.0, -jnp.inf).astype(F32)
        for h in range(N_HEADS):
            hs = slice(h * HEAD_DIM, (h + 1) * HEAD_DIM)
            s_ref[slot, h] = jnp.dot(kbuf[kslot, :, hs], qT_ref[hs, :], preferred_element_type=F32) + madd

    def softmax(slot, kind):
        for h in range(N_HEADS):
            s = s_ref[slot, h]
            if kind == 1:
                s = s + tz_ref[1, h]
            elif kind == 2:
                s = s + tz_ref[0, h]
            m_old = m_ref[h:h + 1, :]
            m_new = jnp.maximum(m_old, jnp.max(s, axis=0, keepdims=True))
            m_safe = jnp.where(m_new == -jnp.inf, 0.0, m_new)
            p = jnp.exp2(s - m_safe)
            alpha = jnp.exp2(m_old - m_safe)
            l_ref[h:h + 1, :] = alpha * l_ref[h:h + 1, :] + jnp.sum(p, axis=0, keepdims=True)
            m_ref[h:h + 1, :] = m_new
            alpha_ref[slot, h:h + 1, :] = alpha
            p_ref[slot, h] = p.astype(MXU_DTYPE)

    def weighted_values(j, slot):
        vslot = lax.rem(j, KV_SLOTS)
        for h in range(N_HEADS):
            hs = slice(h * HEAD_DIM, (h + 1) * HEAD_DIM)
            pv = jnp.dot(vbuf[vslot, hs, :], p_ref[slot, h], preferred_element_type=F32)
            acc_ref[hs, :] = alpha_ref[slot, h:h + 1, :] * acc_ref[hs, :] + pv

    def position(j, slot, kind, has_prev=True, has_next=True):
        if has_prev:
            v_copy(j - 1).wait()
        if has_next:
            k_copy(j + 1).wait()
            v_copy(j + 1).start()

            @pl.when(j + KV_SLOTS < npos)
            def _():
                k_copy(j + KV_SLOTS).start()

        if has_prev:
            weighted_values(j - 1, 1 - slot)
        if has_next:
            logits(j + 1, 1 - slot)
        softmax(slot, kind)

    k_copy(0).wait()
    logits(0, 0)
    position(0, 0, 2, has_prev=False)
    position(1, 1, 1)

    def far_pair(t, c):
        j = 2 + 2 * t
        position(j, 0, 0)
        position(j + 1, 1, 0)
        return c

    lax.fori_loop(0, n_pairs - 1, far_pair, 0)
    position(npos - 2, 0, 0)
    position(npos - 1, 1, 0, has_next=False)
    v_copy(npos - 1).wait()
    weighted_values(npos - 1, 1)

    for h in range(N_HEADS):
        hs = slice(h * HEAD_DIM, (h + 1) * HEAD_DIM)
        out = acc_ref[hs, :] / l_ref[h:h + 1, :]
        yT_ref[hs, :] = out * jax.nn.sigmoid(gaT_ref[hs, :])


def _attention(qiT, wiT, ki, qT, gaT, tab, kn, vT, top_k):
    B, D, Lp = qT.shape
    T = SEQ_TILE
    tr_spec = lambda n: pl.BlockSpec((None, n, T), lambda b, i: (b, 0, i))
    return pl.pallas_call(
        functools.partial(_attn_kernel, top_k),
        grid=(B, Lp // T),
        in_specs=[tr_spec(IDX_HEADS * IDX_PAD), tr_spec(WI_ROWS),
                  pl.BlockSpec((None, Lp, IDX_PAD), lambda b, i: (b, 0, 0), pipeline_mode=pl.Buffered(1)),
                  tr_spec(D), tr_spec(D),
                  pl.BlockSpec((N_HEADS, 2 * T), lambda b, i: (0, 0)),
                  pl.BlockSpec(memory_space=pl.ANY), pl.BlockSpec(memory_space=pl.ANY)],
        out_specs=tr_spec(D),
        out_shape=jax.ShapeDtypeStruct((B, D, Lp), F32),
        scratch_shapes=[pltpu.VMEM((Lp, T), jnp.int32),
                        pltpu.VMEM((KV_SLOTS, T, D), MXU_DTYPE), pltpu.VMEM((KV_SLOTS, D, T), MXU_DTYPE),
                        pltpu.SemaphoreType.DMA((2, KV_SLOTS)),
                        pltpu.VMEM((N_HEADS, T), F32), pltpu.VMEM((N_HEADS, T), F32),
                        pltpu.VMEM((2, N_HEADS, T), F32),
                        pltpu.VMEM((D, T), F32),
                        pltpu.VMEM((2, N_HEADS, T, T), F32), pltpu.VMEM((2, N_HEADS, T, T), MXU_DTYPE),
                        pltpu.VMEM((2, N_HEADS, T, T), F32)],
        compiler_params=pltpu.CompilerParams(dimension_semantics=("arbitrary", "arbitrary"),
                                             vmem_limit_bytes=VMEM_LIMIT_BYTES),
        name="attn",
    )(qiT, wiT, ki, qT, gaT, tab, kn, vT)


def _out_kernel(last, h_ref, yr_ref, yT_ref, wo_ref, g2_ref, w1_ref, w2_ref, fg_ref, o_ref):
    y = yr_ref[...] + yT_ref[...].T
    h1 = h_ref[...] + jnp.dot(y.astype(MXU_DTYPE), wo_ref[...], preferred_element_type=F32)
    hn = _rms(h1, g2_ref[...]).astype(MXU_DTYPE)
    a = jnp.dot(hn, w1_ref[...], preferred_element_type=F32)
    a = jnp.square(jnp.maximum(a, 0.0)).astype(MXU_DTYPE)
    h2 = h1 + jnp.dot(a, w2_ref[...], preferred_element_type=F32)
    if last:
        h2 = _rms(h2, fg_ref[...])
    o_ref[...] = h2


def _out_mlp(h, yr, yT, wo, g2, w1, w2, fg, last):
    B, Lp, D = h.shape
    T = SEQ_TILE
    row_spec = pl.BlockSpec((None, T, D), lambda b, i: (b, i, 0))
    const = lambda shape: pl.BlockSpec(shape, lambda b, i: (0,) * len(shape), pipeline_mode=pl.Buffered(1))
    return pl.pallas_call(
        functools.partial(_out_kernel, last),
        grid=(B, Lp // T),
        in_specs=[row_spec, row_spec, pl.BlockSpec((None, D, T), lambda b, i: (b, 0, i)),
                  const((D, D)), const((1, D)), const((D, D_FF)), const((D_FF, D)), const((1, D))],
        out_specs=row_spec,
        out_shape=jax.ShapeDtypeStruct((B, Lp, D), F32),
        compiler_params=pltpu.CompilerParams(dimension_semantics=("arbitrary", "arbitrary"),
                                             vmem_limit_bytes=VMEM_LIMIT_BYTES),
        name="out_mlp",
    )(h, yr, yT, wo, g2, w1, w2, fg)


def _t5_bucket_table(n):
    max_exact = N_BUCKETS // 2
    d = np.arange(n)
    df = np.maximum(d, 1).astype(np.float64)
    large = max_exact + (np.log(df / max_exact) / math.log(MAX_DISTANCE / max_exact)
                         * (N_BUCKETS - max_exact)).astype(np.int32)
    large = np.minimum(large, N_BUCKETS - 1)
    return np.where(d < max_exact, d, large).astype(np.int32)


def _bias_table(rel_bias):
    T = SEQ_TILE
    buckets = _t5_bucket_table(2 * T)
    assert np.all(buckets[T + 1:] == N_BUCKETS - 1)
    table = rel_bias[buckets] - rel_bias[N_BUCKETS - 1][None, :]
    return (table * LOG2E).T.astype(F32)


def _layer_weights(w_in):
    offs = np.concatenate([[0], np.cumsum(IN_SIZES)])
    parts = [w_in[:, int(offs[i]):int(offs[i + 1])] for i in range(len(IN_SIZES))]
    w_ug, w_ur, w_q, w_k, w_v, w_qi, w_ki, w_wi, w_gr, w_ga = parts
    D = w_in.shape[0]
    w_ki = jnp.pad(w_ki, ((0, 0), (0, IDX_PAD - IDX_DIM)))
    wn = jnp.concatenate([w_ug, w_ur, w_gr, w_k, w_ki], axis=1).astype(MXU_DTYPE)
    w_qi = jnp.pad(w_qi.reshape(D, IDX_HEADS, IDX_DIM), ((0, 0), (0, 0), (0, IDX_PAD - IDX_DIM)))
    w_qi = w_qi.reshape(D, IDX_HEADS * IDX_PAD)
    w_wi = jnp.pad(w_wi, ((0, 0), (0, WI_ROWS - IDX_HEADS)))
    scale = HEAD_DIM ** -0.5 * LOG2E
    wt = jnp.concatenate([w_q * scale, w_v, w_ga, w_qi, w_wi], axis=1).T.astype(MXU_DTYPE)
    return wn, wt


def kernel(x, norm1_g, w_in, conv_w, conv_b, w_rg_a, b_rg_a, w_rg_x, b_rg_x, lru_lambda, w_out, norm2_g,
           w_mlp1, w_mlp2, rel_bias, meta_tokens, final_g):
    B, S, D = x.shape
    depth = w_in.shape[0]
    L = S + N_META
    T = SEQ_TILE
    Lp = -(-L // T) * T
    top_k = min(TOP_K_MAX, L // 4)
    meta = jnp.broadcast_to(meta_tokens[None].astype(x.dtype), (B, N_META, D))
    h = jnp.concatenate([meta, x, jnp.zeros((B, Lp - L, D), x.dtype)], axis=1)
    tab = _bias_table(rel_bias)
    row = lambda v: v.reshape(1, -1)
    for l in range(depth):
        wn, wt = _layer_weights(w_in[l])
        ug, ur, gr, kn, ki, qT, vT, gaT, qiT, wiT = _proj(h, row(norm1_g[l]), wn, wt)
        yr = _rglru(ur, ug, gr, conv_w[l], row(conv_b[l]), w_rg_a[l].astype(MXU_DTYPE), row(b_rg_a[l]),
                    w_rg_x[l].astype(MXU_DTYPE), row(b_rg_x[l]), row(lru_lambda[l]))
        yT = _attention(qiT, wiT, ki, qT, gaT, tab, kn, vT, top_k)
        h = _out_mlp(h, yr, yT, w_out[l].astype(MXU_DTYPE), row(norm2_g[l]), w_mlp1[l].astype(MXU_DTYPE),
                     w_mlp2[l].astype(MXU_DTYPE), row(final_g), last=(l == depth - 1))
    return h[:, N_META:L]
```

```python
import functools
import math

import numpy as np
import jax
import jax.numpy as jnp
from jax import lax
from jax.experimental import pallas as pl
from jax.experimental.pallas import tpu as pltpu

D_MODEL = 1024
N_META = 16
D_RNN = D_MODEL
RG_BLOCKS = 4
RG_BLOCK_W = D_RNN // RG_BLOCKS
CONV_W = 4
LRU_C = 8.0
N_HEADS = 8
HEAD_DIM = D_MODEL // N_HEADS
IDX_HEADS = 8
IDX_DIM = 64
TOP_K_MAX = 256
N_BUCKETS = 32
MAX_DISTANCE = 128
D_FF = 4 * D_MODEL
NORM_EPS = 1e-6
IN_SIZES = (D_RNN, D_RNN, D_MODEL, D_MODEL, D_MODEL, IDX_HEADS * IDX_DIM, IDX_DIM, IDX_HEADS, D_MODEL, D_MODEL)

LANES = 128
SUBLANES = 8
VMEM_LIMIT_BYTES = 56 * 1024 * 1024

SEQ_TILE = 256
IDX_PAD = LANES
WI_ROWS = 16
COUNT_ROWS = 64
KV_SLOTS = 3
V_ROWS = HEAD_DIM + 16

MXU_DTYPE = jnp.bfloat16
F32 = jnp.float32
I16 = jnp.int16
I16_MIN = -(2 ** 15)
I16_MAX = 2 ** 15 - 1
LOG2E = math.log2(math.e)


def _rms(x, g):
    ms = jnp.mean(x * x, axis=-1, keepdims=True)
    return x * lax.rsqrt(ms + NORM_EPS) * g


N_NAT = 4 * D_MODEL + IDX_PAD
N_TR = 3 * D_MODEL + IDX_HEADS * IDX_PAD + WI_ROWS


def _proj_kernel(h_ref, g_ref, wn_ref, wt_ref,
                 ug_ref, ur_ref, gr_ref, k_ref, ki_ref,
                 qT_ref, vT_ref, gaT_ref, qiT_ref, wiT_ref):
    xn = _rms(h_ref[...], g_ref[...]).astype(MXU_DTYPE)

    def nat(c0, n):
        return jnp.dot(xn, wn_ref[:, c0:c0 + n], preferred_element_type=F32)

    def tr(r0, n):
        return lax.dot_general(wt_ref[r0:r0 + n, :], xn, (((1,), (1,)), ((), ())),
                               preferred_element_type=F32)

    ug_ref[...] = nat(0, D_MODEL)
    ur_ref[...] = nat(D_MODEL, D_MODEL)
    gr_ref[...] = nat(2 * D_MODEL, D_MODEL)
    k_ref[...] = nat(3 * D_MODEL, D_MODEL).astype(k_ref.dtype)
    ki_ref[...] = nat(4 * D_MODEL, IDX_PAD).astype(ki_ref.dtype)
    qT_ref[...] = tr(0, D_MODEL).astype(qT_ref.dtype)
    vt = tr(D_MODEL, D_MODEL).astype(vT_ref.dtype)
    for h in range(N_HEADS):
        vT_ref[h * V_ROWS:h * V_ROWS + HEAD_DIM, :] = vt[h * HEAD_DIM:(h + 1) * HEAD_DIM, :]
        vT_ref[h * V_ROWS + HEAD_DIM:(h + 1) * V_ROWS, :] = jnp.ones((V_ROWS - HEAD_DIM, vt.shape[1]), vT_ref.dtype)
    gaT_ref[...] = tr(2 * D_MODEL, D_MODEL)
    qiT_ref[...] = tr(3 * D_MODEL, IDX_HEADS * IDX_PAD).astype(qiT_ref.dtype)
    wiT_ref[...] = tr(3 * D_MODEL + IDX_HEADS * IDX_PAD, WI_ROWS)


def _proj(h, g1, wn, wt):
    B, Lp, D = h.shape
    T = SEQ_TILE
    nat_spec = lambda n: pl.BlockSpec((None, T, n), lambda b, i: (b, i, 0))
    tr_spec = lambda n: pl.BlockSpec((None, n, T), lambda b, i: (b, 0, i))
    const = lambda shape: pl.BlockSpec(shape, lambda b, i: (0,) * len(shape), pipeline_mode=pl.Buffered(1))
    sds = jax.ShapeDtypeStruct
    return pl.pallas_call(
        _proj_kernel,
        grid=(B, Lp // T),
        in_specs=[nat_spec(D), const((1, D)), const((D, N_NAT)), const((N_TR, D))],
        out_specs=[nat_spec(D), nat_spec(D), nat_spec(D), nat_spec(D), nat_spec(IDX_PAD),
                   tr_spec(D), tr_spec(N_HEADS * V_ROWS), tr_spec(D), tr_spec(IDX_HEADS * IDX_PAD),
                   tr_spec(WI_ROWS)],
        out_shape=[sds((B, Lp, D), F32), sds((B, Lp, D), F32), sds((B, Lp, D), F32),
                   sds((B, Lp, D), MXU_DTYPE), sds((B, Lp, IDX_PAD), MXU_DTYPE),
                   sds((B, D, Lp), MXU_DTYPE), sds((B, N_HEADS * V_ROWS, Lp), MXU_DTYPE), sds((B, D, Lp), F32),
                   sds((B, IDX_HEADS * IDX_PAD, Lp), MXU_DTYPE), sds((B, WI_ROWS, Lp), F32)],
        compiler_params=pltpu.CompilerParams(dimension_semantics=("arbitrary", "arbitrary"),
                                             vmem_limit_bytes=VMEM_LIMIT_BYTES),
        name="proj",
    )(h, g1, wn, wt)


def _rglru_kernel(ur_ref, ug_ref, gr_ref, cw_ref, cb_ref, wa_ref, ba_ref, wx_ref, bx_ref, lam_ref,
                  y_ref, ext_ref, hst_ref):
    T = SEQ_TILE

    @pl.when(pl.program_id(1) == 0)
    def _():
        ext_ref[0:SUBLANES, :] = jnp.zeros((SUBLANES, D_RNN), F32)
        hst_ref[...] = jnp.zeros_like(hst_ref)

    x = ur_ref[...]
    ext_ref[SUBLANES:SUBLANES + T, :] = x
    base = SUBLANES - (CONV_W - 1)
    xc = cw_ref[0:1, :] * ext_ref[base:base + T, :]
    for j in range(1, CONV_W):
        xc = xc + cw_ref[j:j + 1, :] * ext_ref[base + j:base + j + T, :]
    xc = xc + cb_ref[...]
    ext_ref[0:SUBLANES, :] = x[T - SUBLANES:T, :]

    xcb = xc.astype(MXU_DTYPE)

    def block_diag(w_ref, b_ref):
        parts = [jnp.dot(xcb[:, g * RG_BLOCK_W:(g + 1) * RG_BLOCK_W], w_ref[g], preferred_element_type=F32)
                 for g in range(RG_BLOCKS)]
        return jnp.concatenate(parts, axis=-1) + b_ref[...]

    r = jax.nn.sigmoid(block_diag(wa_ref, ba_ref))
    gate_i = jax.nn.sigmoid(block_diag(wx_ref, bx_ref))
    z = -lam_ref[...]
    softplus = jnp.maximum(z, 0.0) + jnp.log1p(jnp.exp(-jnp.abs(z)))
    log_a = (-LRU_C) * r * softplus
    a = jnp.exp(log_a)
    u = jnp.sqrt(-jnp.tanh(log_a) * (a * a + 1.0)) * (gate_i * xc)

    row = lax.broadcasted_iota(jnp.int32, (T, D_RNN), 0)
    s = 1
    while s < T:
        a_prev = pltpu.roll(a, s, axis=0)
        u_prev = pltpu.roll(u, s, axis=0)
        valid = row >= s
        u = jnp.where(valid, a * u_prev + u, u)
        a = jnp.where(valid, a * a_prev, a)
        s *= 2
    h = a * hst_ref[...] + u
    hst_ref[...] = h[T - 1:T, :]
    y_ref[...] = h * jax.nn.gelu(ug_ref[...]) * jax.nn.sigmoid(gr_ref[...])


def _rglru(ur, ug, gr, cw, cb, wa, ba, wx, bx, lam):
    B, Lp, D = ur.shape
    T = SEQ_TILE
    row_spec = pl.BlockSpec((None, T, D), lambda b, i: (b, i, 0))
    const = lambda shape: pl.BlockSpec(shape, lambda b, i: (0,) * len(shape))
    return pl.pallas_call(
        _rglru_kernel,
        grid=(B, Lp // T),
        in_specs=[row_spec, row_spec, row_spec, const((CONV_W, D)), const((1, D)),
                  const((RG_BLOCKS, RG_BLOCK_W, RG_BLOCK_W)), const((1, D)),
                  const((RG_BLOCKS, RG_BLOCK_W, RG_BLOCK_W)), const((1, D)), const((1, D))],
        out_specs=row_spec,
        out_shape=jax.ShapeDtypeStruct((B, Lp, D), F32),
        scratch_shapes=[pltpu.VMEM((SUBLANES + T, D), F32), pltpu.VMEM((1, D), F32)],
        compiler_params=pltpu.CompilerParams(dimension_semantics=("arbitrary", "arbitrary"),
                                             vmem_limit_bytes=VMEM_LIMIT_BYTES),
        name="rglru",
    )(ur, ug, gr, cw, cb, wa, ba, wx, bx, lam)


def _to_sort_key(s):
    bits = lax.bitcast_convert_type(s, jnp.int32)
    return bits ^ ((bits >> 31) & jnp.int32(0x7FFFFFFF))


def _attn_kernel(top_k, qiT_ref, wiT_ref, ki_ref, qT_ref, gaT_ref, tab_ref, kn_hbm, vT_hbm,
                 yT_ref,
                 hi_ref, lo_ref, kbuf, vbuf, sem, m_ref, acc_ref, tz_ref,
                 s0_ref, s1_ref, p0_ref, p1_ref, alpha0_ref, alpha1_ref, smax0_ref, smax1_ref):
    T = SEQ_TILE
    s_refs, p_refs, alpha_refs = (s0_ref, s1_ref), (p0_ref, p1_ref), (alpha0_ref, alpha1_ref)
    smax_refs = (smax0_ref, smax1_ref)
    b = pl.program_id(0)
    qb = pl.program_id(1)
    nblk = qb + 1

    n_pairs = jnp.maximum((nblk - 1) // 2, 1)
    npos = 2 + 2 * n_pairs

    def tile_of(j):
        return jnp.maximum(nblk - 1 - j, 0)

    def k_copy(j):
        slot = lax.rem(j, KV_SLOTS)
        return pltpu.make_async_copy(kn_hbm.at[b, pl.ds(tile_of(j) * T, T), :], kbuf.at[slot], sem.at[0, slot])

    def v_copy(j):
        slot = lax.rem(j, KV_SLOTS)
        return pltpu.make_async_copy(vT_hbm.at[b, :, pl.ds(tile_of(j) * T, T)], vbuf.at[slot], sem.at[1, slot])

    v_copy(0).start()
    for j in range(KV_SLOTS):
        k_copy(j).start()

    @pl.when((b == 0) & (qb == 0))
    def _():
        for h in range(N_HEADS):
            rows = jnp.broadcast_to(tab_ref[h:h + 1, :], (T, 2 * T))
            skew = pltpu.roll(rows, 0, axis=1, stride=1, stride_axis=0)
            tz_ref[0, h] = skew[:, 0:T]
            tz_ref[1, h] = skew[:, T:2 * T]

    def score_tile(kb):
        kib = ki_ref[pl.ds(pl.multiple_of(kb * T, T), T), :]
        acc = jnp.zeros((T, T), F32)
        for h in range(IDX_HEADS):
            s = jnp.dot(kib, qiT_ref[h * IDX_PAD:(h + 1) * IDX_PAD, :], preferred_element_type=F32)
            acc = acc + jnp.maximum(s, 0.0) * wiT_ref[h:h + 1, :]
        return acc

    def key_halves(s):
        key = _to_sort_key(s)
        return key >> 16, ((key ^ jnp.int32(0x8000)) << 16) >> 16

    def far_scores(kb, c):
        rows = pl.ds(pl.multiple_of(kb * T, T), T)
        hi, lo = key_halves(score_tile(kb))
        hi_ref[rows, :] = hi.astype(I16)
        lo_ref[rows, :] = lo.astype(I16)
        return c

    lax.fori_loop(0, qb, far_scores, 0)
    kpos = lax.broadcasted_iota(jnp.int32, (T, T), 0)
    qpos = lax.broadcasted_iota(jnp.int32, (T, T), 1)
    hi, lo = key_halves(score_tile(qb))
    diag_rows = pl.ds(pl.multiple_of(qb * T, T), T)
    hi_ref[diag_rows, :] = jnp.where(kpos <= qpos, hi, I16_MIN).astype(I16)
    lo_ref[diag_rows, :] = lo.astype(I16)

    def rows16(v):
        return jnp.broadcast_to(v.astype(I16), (COUNT_ROWS, T))

    def count_rows(pred):
        def body(kb, acc):
            base = pl.multiple_of(kb * T, T)
            for j in range(T // COUNT_ROWS):
                acc = acc + jnp.where(pred(base + j * COUNT_ROWS), jnp.int16(1), jnp.int16(0))
            return acc

        acc = lax.fori_loop(0, nblk, body, jnp.zeros((COUNT_ROWS, T), I16))
        return jnp.sum(acc.astype(jnp.int32), axis=0, keepdims=True)

    def count_ge(ref, trial):
        t = rows16(trial)
        return count_rows(lambda r0: ref[pl.ds(r0, COUNT_ROWS), :] >= t)

    def kth_largest(ref, k):
        zero = jnp.zeros((1, T), jnp.int32)
        v = jnp.where(count_ge(ref, zero) >= k, zero, jnp.int32(I16_MIN))

        def bit_step(i, v):
            trial = v + lax.shift_left(jnp.int32(1), jnp.int32(14) - i)
            return jnp.where(count_ge(ref, trial) >= k, trial, v)

        return lax.fori_loop(0, 15, bit_step, v)

    hi_tau = jnp.maximum(kth_largest(hi_ref, top_k), jnp.int32(I16_MIN + 1))
    n_hi_ge = count_ge(hi_ref, hi_tau)
    n_hi_gt = jnp.where(hi_tau == I16_MAX, 0, count_ge(hi_ref, hi_tau + 1))
    need_lo = top_k - n_hi_gt

    def bucket_lo(kb, c):
        base = pl.multiple_of(kb * T, T)
        ht = rows16(hi_tau)
        for j in range(T // COUNT_ROWS):
            rows = pl.ds(base + j * COUNT_ROWS, COUNT_ROWS)
            lo_ref[rows, :] = jnp.where(hi_ref[rows, :] == ht, lo_ref[rows, :], jnp.int16(I16_MIN))
        return c

    lax.fori_loop(0, nblk, bucket_lo, 0)
    lo_tau = kth_largest(lo_ref, need_lo)

    n_lo_gt = jnp.where(lo_tau == I16_MAX, 0, count_ge(lo_ref, lo_tau + 1))
    n_lo_ge = jnp.where(lo_tau == I16_MIN, n_hi_ge - n_hi_gt, count_ge(lo_ref, lo_tau))
    need = need_lo - n_lo_gt
    excess = (n_lo_ge - n_lo_gt) > need

    @pl.when(jnp.max(excess.astype(jnp.int32)) > 0)
    def _():
        row_iota = lax.broadcasted_iota(jnp.int32, (COUNT_ROWS, T), 0)
        ht = rows16(hi_tau)
        lt = rows16(lo_tau)

        def is_tie(r0):
            rows = pl.ds(r0, COUNT_ROWS)
            return (hi_ref[rows, :] == ht) & (lo_ref[rows, :] == lt)

        def count_ties_upto(cut):
            cut_b = jnp.broadcast_to(cut, (COUNT_ROWS, T))
            return count_rows(lambda r0: is_tie(r0) & (row_iota + r0 <= cut_b))

        n_bits = max(1, int(hi_ref.shape[0] - 1).bit_length())

        def idx_step(i, cut):
            trial = cut - lax.shift_left(jnp.int32(1), jnp.int32(n_bits - 1) - i)
            return jnp.where(count_ties_upto(trial) >= need, trial, cut)

        cut = lax.fori_loop(0, n_bits, idx_step, jnp.full((1, T), 2 ** n_bits - 1, jnp.int32))
        cut = jnp.where(excess, cut, jnp.int32(2 ** 31 - 1))
        cut_b = jnp.broadcast_to(cut, (COUNT_ROWS, T))

        def demote(kb, c):
            base = pl.multiple_of(kb * T, T)
            for j in range(T // COUNT_ROWS):
                r0 = base + j * COUNT_ROWS
                rows = pl.ds(r0, COUNT_ROWS)
                drop = is_tie(r0) & (row_iota + r0 > cut_b)
                hi_ref[rows, :] = jnp.where(drop, jnp.int16(I16_MIN), hi_ref[rows, :])
            return c

        lax.fori_loop(0, nblk, demote, 0)

    m_ref[...] = jnp.full(m_ref.shape, -jnp.inf, F32)
    acc_ref[...] = jnp.zeros_like(acc_ref)

    def mask_add(j):
        rows = pl.ds(pl.multiple_of(tile_of(j) * T, T), T)
        hi_j = jnp.where(j < nblk, hi_tau, jnp.int32(I16_MAX))
        ht = jnp.broadcast_to(hi_j.astype(I16), (T, T))
        lt = jnp.broadcast_to(lo_tau.astype(I16), (T, T))
        hi = hi_ref[rows, :]
        one, zero = jnp.int16(1), jnp.int16(0)
        keep = jnp.where(hi > ht, one, jnp.where(hi == ht, jnp.where(lo_ref[rows, :] >= lt, one, zero), zero))
        return jnp.where(keep.astype(jnp.int32) > 0, 0.0, -jnp.inf).astype(F32)

    def logits(j, slot, kind, h, madd):
        hs = slice(h * HEAD_DIM, (h + 1) * HEAD_DIM)
        s = jnp.dot(kbuf[lax.rem(j, KV_SLOTS), :, hs], qT_ref[hs, :], preferred_element_type=F32) + madd
        if kind == 1:
            s = s + tz_ref[1, h]
        elif kind == 2:
            s = s + tz_ref[0, h]
        s_refs[slot][h] = s
        smax_refs[slot][h:h + 1, :] = jnp.max(s, axis=0, keepdims=True)

    def softmax(slot, h):
        s = s_refs[slot][h]
        m_old = m_ref[h:h + 1, :]
        m_new = jnp.maximum(m_old, smax_refs[slot][h:h + 1, :])
        m_safe = jnp.where(m_new == -jnp.inf, 0.0, m_new)
        p = jnp.exp2(s - m_safe)
        alpha = jnp.exp2(m_old - m_safe)
        m_ref[h:h + 1, :] = m_new
        alpha_refs[slot][h:h + 1, :] = alpha
        p_refs[slot][h] = p.astype(MXU_DTYPE)

    def weighted_values(j, slot, h):
        hv = slice(h * V_ROWS, (h + 1) * V_ROWS)
        pv = jnp.dot(vbuf[lax.rem(j, KV_SLOTS), hv, :], p_refs[slot][h], preferred_element_type=F32)
        acc_ref[hv, :] = alpha_refs[slot][h:h + 1, :] * acc_ref[hv, :] + pv

    def position(j, slot, kind, has_prev=True, has_next=True):
        if has_prev:
            v_copy(j - 1).wait()
        if has_next:
            k_copy(j + 1).wait()
            v_copy(j + 1).start()

            @pl.when(j + KV_SLOTS < npos)
            def _():
                k_copy(j + KV_SLOTS).start()

            madd = mask_add(j + 1)
        next_kind = 1 if kind == 2 else 0
        for h in range(N_HEADS):
            softmax(slot, h)
            if has_prev:
                weighted_values(j - 1, 1 - slot, h)
            if has_next:
                logits(j + 1, 1 - slot, next_kind, h, madd)

    k_copy(0).wait()
    madd0 = mask_add(0)
    for h in range(N_HEADS):
        logits(0, 0, 2, h, madd0)
    position(0, 0, 2, has_prev=False)
    position(1, 1, 1)

    def far_pair(t, c):
        j = 2 + 2 * t
        position(j, 0, 0)
        position(j + 1, 1, 0)
        return c

    lax.fori_loop(0, n_pairs - 1, far_pair, 0)
    position(npos - 2, 0, 0)
    position(npos - 1, 1, 0, has_next=False)
    v_copy(npos - 1).wait()
    for h in range(N_HEADS):
        weighted_values(npos - 1, 1, h)

    for h in range(N_HEADS):
        hs = slice(h * HEAD_DIM, (h + 1) * HEAD_DIM)
        out = acc_ref[h * V_ROWS:h * V_ROWS + HEAD_DIM, :] / acc_ref[h * V_ROWS + HEAD_DIM:h * V_ROWS + HEAD_DIM + 1, :]
        yT_ref[hs, :] = out * jax.nn.sigmoid(gaT_ref[hs, :])


def _attention(qiT, wiT, ki, qT, gaT, tab, kn, vT, top_k):
    B, D, Lp = qT.shape
    T = SEQ_TILE
    tr_spec = lambda n: pl.BlockSpec((None, n, T), lambda b, i: (b, 0, i))
    return pl.pallas_call(
        functools.partial(_attn_kernel, top_k),
        grid=(B, Lp // T),
        in_specs=[tr_spec(IDX_HEADS * IDX_PAD), tr_spec(WI_ROWS),
                  pl.BlockSpec((None, Lp, IDX_PAD), lambda b, i: (b, 0, 0), pipeline_mode=pl.Buffered(1)),
                  tr_spec(D), tr_spec(D),
                  pl.BlockSpec((N_HEADS, 2 * T), lambda b, i: (0, 0)),
                  pl.BlockSpec(memory_space=pl.ANY), pl.BlockSpec(memory_space=pl.ANY)],
        out_specs=tr_spec(D),
        out_shape=jax.ShapeDtypeStruct((B, D, Lp), F32),
        scratch_shapes=[pltpu.VMEM((Lp, T), I16), pltpu.VMEM((Lp, T), I16),
                        pltpu.VMEM((KV_SLOTS, T, D), MXU_DTYPE),
                        pltpu.VMEM((KV_SLOTS, N_HEADS * V_ROWS, T), MXU_DTYPE),
                        pltpu.SemaphoreType.DMA((2, KV_SLOTS)),
                        pltpu.VMEM((N_HEADS, T), F32),
                        pltpu.VMEM((N_HEADS * V_ROWS, T), F32),
                        pltpu.VMEM((2, N_HEADS, T, T), F32),
                        pltpu.VMEM((N_HEADS, T, T), F32), pltpu.VMEM((N_HEADS, T, T), F32),
                        pltpu.VMEM((N_HEADS, T, T), MXU_DTYPE), pltpu.VMEM((N_HEADS, T, T), MXU_DTYPE),
                        pltpu.VMEM((N_HEADS, T), F32), pltpu.VMEM((N_HEADS, T), F32),
                        pltpu.VMEM((N_HEADS, T), F32), pltpu.VMEM((N_HEADS, T), F32)],
        compiler_params=pltpu.CompilerParams(dimension_semantics=("arbitrary", "arbitrary"),
                                             vmem_limit_bytes=VMEM_LIMIT_BYTES),
        name="attn",
    )(qiT, wiT, ki, qT, gaT, tab, kn, vT)


def _out_kernel(last, h_ref, yr_ref, yT_ref, wo_ref, g2_ref, w1_ref, w2_ref, fg_ref, o_ref):
    y = yr_ref[...] + yT_ref[...].T
    h1 = h_ref[...] + jnp.dot(y.astype(MXU_DTYPE), wo_ref[...], preferred_element_type=F32)
    hn = _rms(h1, g2_ref[...]).astype(MXU_DTYPE)
    a = jnp.dot(hn, w1_ref[...], preferred_element_type=F32)
    a = jnp.square(jnp.maximum(a, 0.0)).astype(MXU_DTYPE)
    h2 = h1 + jnp.dot(a, w2_ref[...], preferred_element_type=F32)
    if last:
        h2 = _rms(h2, fg_ref[...])
    o_ref[...] = h2


def _out_mlp(h, yr, yT, wo, g2, w1, w2, fg, last):
    B, Lp, D = h.shape
    T = SEQ_TILE
    row_spec = pl.BlockSpec((None, T, D), lambda b, i: (b, i, 0))
    const = lambda shape: pl.BlockSpec(shape, lambda b, i: (0,) * len(shape), pipeline_mode=pl.Buffered(1))
    return pl.pallas_call(
        functools.partial(_out_kernel, last),
        grid=(B, Lp // T),
        in_specs=[row_spec, row_spec, pl.BlockSpec((None, D, T), lambda b, i: (b, 0, i)),
                  const((D, D)), const((1, D)), const((D, D_FF)), const((D_FF, D)), const((1, D))],
        out_specs=row_spec,
        out_shape=jax.ShapeDtypeStruct((B, Lp, D), F32),
        compiler_params=pltpu.CompilerParams(dimension_semantics=("arbitrary", "arbitrary"),
                                             vmem_limit_bytes=VMEM_LIMIT_BYTES),
        name="out_mlp",
    )(h, yr, yT, wo, g2, w1, w2, fg)


def _t5_bucket_table(n):
    max_exact = N_BUCKETS // 2
    d = np.arange(n)
    df = np.maximum(d, 1).astype(np.float64)
    large = max_exact + (np.log(df / max_exact) / math.log(MAX_DISTANCE / max_exact)
                         * (N_BUCKETS - max_exact)).astype(np.int32)
    large = np.minimum(large, N_BUCKETS - 1)
    return np.where(d < max_exact, d, large).astype(np.int32)


def _bias_table(rel_bias):
    T = SEQ_TILE
    buckets = _t5_bucket_table(2 * T)
    assert np.all(buckets[T + 1:] == N_BUCKETS - 1)
    table = rel_bias[buckets] - rel_bias[N_BUCKETS - 1][None, :]
    return (table * LOG2E).T.astype(F32)


def _layer_weights(w_in):
    offs = np.concatenate([[0], np.cumsum(IN_SIZES)])
    parts = [w_in[:, int(offs[i]):int(offs[i + 1])] for i in range(len(IN_SIZES))]
    w_ug, w_ur, w_q, w_k, w_v, w_qi, w_ki, w_wi, w_gr, w_ga = parts
    D = w_in.shape[0]
    w_ki = jnp.pad(w_ki, ((0, 0), (0, IDX_PAD - IDX_DIM)))
    wn = jnp.concatenate([w_ug, w_ur, w_gr, w_k, w_ki], axis=1).astype(MXU_DTYPE)
    w_qi = jnp.pad(w_qi.reshape(D, IDX_HEADS, IDX_DIM), ((0, 0), (0, 0), (0, IDX_PAD - IDX_DIM)))
    w_qi = w_qi.reshape(D, IDX_HEADS * IDX_PAD)
    w_wi = jnp.pad(w_wi, ((0, 0), (0, WI_ROWS - IDX_HEADS)))
    scale = HEAD_DIM ** -0.5 * LOG2E
    wt = jnp.concatenate([w_q * scale, w_v, w_ga, w_qi, w_wi], axis=1).T.astype(MXU_DTYPE)
    return wn, wt


def kernel(x, norm1_g, w_in, conv_w, conv_b, w_rg_a, b_rg_a, w_rg_x, b_rg_x, lru_lambda, w_out, norm2_g,
           w_mlp1, w_mlp2, rel_bias, meta_tokens, final_g):
    B, S, D = x.shape
    depth = w_in.shape[0]
    L = S + N_META
    T = SEQ_TILE
    Lp = -(-L // T) * T
    top_k = min(TOP_K_MAX, L // 4)
    meta = jnp.broadcast_to(meta_tokens[None].astype(x.dtype), (B, N_META, D))
    h = jnp.concatenate([meta, x, jnp.zeros((B, Lp - L, D), x.dtype)], axis=1)
    tab = _bias_table(rel_bias)
    row = lambda v: v.reshape(1, -1)
    for l in range(depth):
        wn, wt = _layer_weights(w_in[l])
        ug, ur, gr, kn, ki, qT, vT, gaT, qiT, wiT = _proj(h, row(norm1_g[l]), wn, wt)
        yr = _rglru(ur, ug, gr, conv_w[l], row(conv_b[l]), w_rg_a[l].astype(MXU_DTYPE), row(b_rg_a[l]),
                    w_rg_x[l].astype(MXU_DTYPE), row(b_rg_x[l]), row(lru_lambda[l]))
        yT = _attention(qiT, wiT, ki, qT, gaT, tab, kn, vT, top_k)
        h = _out_mlp(h, yr, yT, w_out[l].astype(MXU_DTYPE), row(norm2_g[l]), w_mlp1[l].astype(MXU_DTYPE),
                     w_mlp2[l].astype(MXU_DTYPE), row(final_g), last=(l == depth - 1))
    return h[:, N_META:L]
```

```python
import functools
import math

import numpy as np
import jax
import jax.numpy as jnp
from jax import lax
from jax.experimental import pallas as pl
from jax.experimental.pallas import tpu as pltpu

D_MODEL = 1024
N_META = 16
D_RNN = D_MODEL
RG_BLOCKS = 4
RG_BLOCK_W = D_RNN // RG_BLOCKS
CONV_W = 4
LRU_C = 8.0
N_HEADS = 8
HEAD_DIM = D_MODEL // N_HEADS
IDX_HEADS = 8
IDX_DIM = 64
TOP_K_MAX = 256
N_BUCKETS = 32
MAX_DISTANCE = 128
D_FF = 4 * D_MODEL
NORM_EPS = 1e-6
IN_SIZES = (D_RNN, D_RNN, D_MODEL, D_MODEL, D_MODEL, IDX_HEADS * IDX_DIM, IDX_DIM, IDX_HEADS, D_MODEL, D_MODEL)

LANES = 128
SUBLANES = 8
VMEM_LIMIT_BYTES = 56 * 1024 * 1024

SEQ_TILE = 256
IDX_PAD = LANES
WI_ROWS = 16
COUNT_ROWS = 64
KV_SLOTS = 6
V_ROWS = HEAD_DIM + 16

MXU_DTYPE = jnp.bfloat16
F32 = jnp.float32
I16 = jnp.int16
I16_MIN = -(2 ** 15)
I16_MAX = 2 ** 15 - 1
LOG2E = math.log2(math.e)


def _rms(x, g):
    ms = jnp.mean(x * x, axis=-1, keepdims=True)
    return x * lax.rsqrt(ms + NORM_EPS) * g


N_NAT = 4 * D_MODEL + IDX_PAD
N_TR = 3 * D_MODEL + IDX_HEADS * IDX_PAD + WI_ROWS


def _proj_kernel(h_ref, g_ref, wn_ref, wt_ref,
                 ug_ref, ur_ref, gr_ref, k_ref, ki_ref,
                 qT_ref, vT_ref, gaT_ref, qiT_ref, wiT_ref):
    xn = _rms(h_ref[...], g_ref[...]).astype(MXU_DTYPE)

    def nat(c0, n):
        return jnp.dot(xn, wn_ref[:, c0:c0 + n], preferred_element_type=F32)

    def tr(r0, n):
        return lax.dot_general(wt_ref[r0:r0 + n, :], xn, (((1,), (1,)), ((), ())),
                               preferred_element_type=F32)

    ug_ref[...] = nat(0, D_MODEL)
    ur_ref[...] = nat(D_MODEL, D_MODEL)
    gr_ref[...] = nat(2 * D_MODEL, D_MODEL)
    k_ref[...] = nat(3 * D_MODEL, D_MODEL).astype(k_ref.dtype)
    ki_ref[...] = nat(4 * D_MODEL, IDX_PAD).astype(ki_ref.dtype)
    qT_ref[...] = tr(0, D_MODEL).astype(qT_ref.dtype)
    vt = tr(D_MODEL, D_MODEL).astype(vT_ref.dtype)
    for h in range(N_HEADS):
        vT_ref[h * V_ROWS:h * V_ROWS + HEAD_DIM, :] = vt[h * HEAD_DIM:(h + 1) * HEAD_DIM, :]
        vT_ref[h * V_ROWS + HEAD_DIM:(h + 1) * V_ROWS, :] = jnp.ones((V_ROWS - HEAD_DIM, vt.shape[1]), vT_ref.dtype)
    gaT_ref[...] = tr(2 * D_MODEL, D_MODEL)
    qiT_ref[...] = tr(3 * D_MODEL, IDX_HEADS * IDX_PAD).astype(qiT_ref.dtype)
    wiT_ref[...] = tr(3 * D_MODEL + IDX_HEADS * IDX_PAD, WI_ROWS)


def _proj(h, g1, wn, wt):
    B, Lp, D = h.shape
    T = SEQ_TILE
    nat_spec = lambda n: pl.BlockSpec((None, T, n), lambda b, i: (b, i, 0))
    tr_spec = lambda n: pl.BlockSpec((None, n, T), lambda b, i: (b, 0, i))
    const = lambda shape: pl.BlockSpec(shape, lambda b, i: (0,) * len(shape), pipeline_mode=pl.Buffered(1))
    sds = jax.ShapeDtypeStruct
    return pl.pallas_call(
        _proj_kernel,
        grid=(B, Lp // T),
        in_specs=[nat_spec(D), const((1, D)), const((D, N_NAT)), const((N_TR, D))],
        out_specs=[nat_spec(D), nat_spec(D), nat_spec(D), nat_spec(D), nat_spec(IDX_PAD),
                   tr_spec(D), pl.BlockSpec((None, None, N_HEADS * V_ROWS, T), lambda b, i: (b, i, 0, 0)),
                   tr_spec(D), tr_spec(IDX_HEADS * IDX_PAD),
                   tr_spec(WI_ROWS)],
        out_shape=[sds((B, Lp, D), F32), sds((B, Lp, D), F32), sds((B, Lp, D), F32),
                   sds((B, Lp, D), MXU_DTYPE), sds((B, Lp, IDX_PAD), MXU_DTYPE),
                   sds((B, D, Lp), MXU_DTYPE), sds((B, Lp // T, N_HEADS * V_ROWS, T), MXU_DTYPE),
                   sds((B, D, Lp), F32),
                   sds((B, IDX_HEADS * IDX_PAD, Lp), MXU_DTYPE), sds((B, WI_ROWS, Lp), F32)],
        compiler_params=pltpu.CompilerParams(dimension_semantics=("arbitrary", "arbitrary"),
                                             vmem_limit_bytes=VMEM_LIMIT_BYTES),
        name="proj",
    )(h, g1, wn, wt)


def _rglru_kernel(ur_ref, ug_ref, gr_ref, cw_ref, cb_ref, wa_ref, ba_ref, wx_ref, bx_ref, lam_ref,
                  y_ref, ext_ref, hst_ref):
    T = SEQ_TILE

    @pl.when(pl.program_id(1) == 0)
    def _():
        ext_ref[0:SUBLANES, :] = jnp.zeros((SUBLANES, D_RNN), F32)
        hst_ref[...] = jnp.zeros_like(hst_ref)

    x = ur_ref[...]
    ext_ref[SUBLANES:SUBLANES + T, :] = x
    base = SUBLANES - (CONV_W - 1)
    xc = cw_ref[0:1, :] * ext_ref[base:base + T, :]
    for j in range(1, CONV_W):
        xc = xc + cw_ref[j:j + 1, :] * ext_ref[base + j:base + j + T, :]
    xc = xc + cb_ref[...]
    ext_ref[0:SUBLANES, :] = x[T - SUBLANES:T, :]

    xcb = xc.astype(MXU_DTYPE)

    def block_diag(w_ref, b_ref):
        parts = [jnp.dot(xcb[:, g * RG_BLOCK_W:(g + 1) * RG_BLOCK_W], w_ref[g], preferred_element_type=F32)
                 for g in range(RG_BLOCKS)]
        return jnp.concatenate(parts, axis=-1) + b_ref[...]

    r = jax.nn.sigmoid(block_diag(wa_ref, ba_ref))
    gate_i = jax.nn.sigmoid(block_diag(wx_ref, bx_ref))
    z = -lam_ref[...]
    softplus = jnp.maximum(z, 0.0) + jnp.log1p(jnp.exp(-jnp.abs(z)))
    log_a = (-LRU_C) * r * softplus
    a = jnp.exp(log_a)
    u = jnp.sqrt(-jnp.tanh(log_a) * (a * a + 1.0)) * (gate_i * xc)

    row = lax.broadcasted_iota(jnp.int32, (T, D_RNN), 0)
    s = 1
    while s < T:
        a_prev = pltpu.roll(a, s, axis=0)
        u_prev = pltpu.roll(u, s, axis=0)
        valid = row >= s
        u = jnp.where(valid, a * u_prev + u, u)
        a = jnp.where(valid, a * a_prev, a)
        s *= 2
    h = a * hst_ref[...] + u
    hst_ref[...] = h[T - 1:T, :]
    y_ref[...] = h * jax.nn.gelu(ug_ref[...]) * jax.nn.sigmoid(gr_ref[...])


def _rglru(ur, ug, gr, cw, cb, wa, ba, wx, bx, lam):
    B, Lp, D = ur.shape
    T = SEQ_TILE
    row_spec = pl.BlockSpec((None, T, D), lambda b, i: (b, i, 0))
    const = lambda shape: pl.BlockSpec(shape, lambda b, i: (0,) * len(shape))
    return pl.pallas_call(
        _rglru_kernel,
        grid=(B, Lp // T),
        in_specs=[row_spec, row_spec, row_spec, const((CONV_W, D)), const((1, D)),
                  const((RG_BLOCKS, RG_BLOCK_W, RG_BLOCK_W)), const((1, D)),
                  const((RG_BLOCKS, RG_BLOCK_W, RG_BLOCK_W)), const((1, D)), const((1, D))],
        out_specs=row_spec,
        out_shape=jax.ShapeDtypeStruct((B, Lp, D), F32),
        scratch_shapes=[pltpu.VMEM((SUBLANES + T, D), F32), pltpu.VMEM((1, D), F32)],
        compiler_params=pltpu.CompilerParams(dimension_semantics=("arbitrary", "arbitrary"),
                                             vmem_limit_bytes=VMEM_LIMIT_BYTES),
        name="rglru",
    )(ur, ug, gr, cw, cb, wa, ba, wx, bx, lam)


def _to_sort_key(s):
    bits = lax.bitcast_convert_type(s, jnp.int32)
    return bits ^ ((bits >> 31) & jnp.int32(0x7FFFFFFF))


def _attn_kernel(top_k, qiT_ref, wiT_ref, ki_ref, qT_ref, gaT_ref, tab_ref, kn_hbm, vT_hbm,
                 yT_ref,
                 hi_ref, lo_ref, kbuf, vbuf, sem, m_ref, acc_ref, tz_ref,
                 s0_ref, s1_ref, p0_ref, p1_ref, alpha0_ref, alpha1_ref, smax0_ref, smax1_ref, si0_ref, si1_ref):
    T = SEQ_TILE
    s_refs, p_refs, alpha_refs = (s0_ref, s1_ref), (p0_ref, p1_ref), (alpha0_ref, alpha1_ref)
    smax_refs, si_refs = (smax0_ref, smax1_ref), (si0_ref, si1_ref)
    b = pl.program_id(0)
    qb = pl.program_id(1)
    nblk = qb + 1

    n_pairs = jnp.maximum((nblk - 1) // 2, 1)
    npos = 2 + 2 * n_pairs

    def tile_of(j):
        return jnp.maximum(nblk - 1 - j, 0)

    def k_copy(j):
        slot = lax.rem(j, KV_SLOTS)
        return pltpu.make_async_copy(kn_hbm.at[b, pl.ds(tile_of(j) * T, T), :], kbuf.at[slot], sem.at[0, slot])

    def v_copy(j):
        slot = lax.rem(j, KV_SLOTS)
        return pltpu.make_async_copy(vT_hbm.at[b, tile_of(j)], vbuf.at[slot], sem.at[1, slot])

    for j in range(KV_SLOTS):
        @pl.when(j < npos)
        def _():
            k_copy(j).start()

    for j in range(KV_SLOTS - 2):
        @pl.when(j < npos)
        def _():
            v_copy(j).start()

    @pl.when((b == 0) & (qb == 0))
    def _():
        for h in range(N_HEADS):
            rows = jnp.broadcast_to(tab_ref[h:h + 1, :], (T, 2 * T))
            skew = pltpu.roll(rows, 0, axis=1, stride=1, stride_axis=0)
            tz_ref[0, h] = skew[:, 0:T]
            tz_ref[1, h] = skew[:, T:2 * T]

    n_even = nblk + lax.rem(nblk, 2)

    def idx_tile(c):
        return jnp.where(c < nblk, c, 0)

    def idx_dot(c, slot, h):
        kib = ki_ref[pl.ds(pl.multiple_of(idx_tile(c) * T, T), T), :]
        si_refs[slot][h] = jnp.dot(kib, qiT_ref[h * IDX_PAD:(h + 1) * IDX_PAD, :], preferred_element_type=F32)

    key_minus_query = lax.broadcasted_iota(jnp.int32, (T, T), 0) - lax.broadcasted_iota(jnp.int32, (T, T), 1)

    def idx_step(c, slot, has_next=True):
        acc = jnp.zeros((T, T), F32)
        for h in range(IDX_HEADS):
            if has_next:
                idx_dot(c + 1, 1 - slot, h)
            acc = acc + jnp.maximum(si_refs[slot][h], 0.0) * wiT_ref[h:h + 1, :]
        kb = idx_tile(c)
        key = _to_sort_key(acc)
        hi = key >> 16
        lo = ((key ^ jnp.int32(0x8000)) << 16) >> 16
        hi = jnp.where(key_minus_query <= (qb - kb) * T, hi, I16_MIN)
        rows = pl.ds(pl.multiple_of(kb * T, T), T)
        hi_ref[rows, :] = hi.astype(I16)
        lo_ref[rows, :] = lo.astype(I16)

    for h in range(IDX_HEADS):
        idx_dot(0, 0, h)

    def idx_pair(t, c):
        idx_step(2 * t, 0)
        idx_step(2 * t + 1, 1)
        return c

    lax.fori_loop(0, n_even // 2 - 1, idx_pair, 0)
    idx_step(n_even - 2, 0)
    idx_step(n_even - 1, 1, has_next=False)

    def rows16(v):
        return jnp.broadcast_to(v.astype(I16), (COUNT_ROWS, T))

    def count_rows(pred):
        def body(kb, acc):
            base = pl.multiple_of(kb * T, T)
            for j in range(T // COUNT_ROWS):
                acc = acc + jnp.where(pred(base + j * COUNT_ROWS), jnp.int16(1), jnp.int16(0))
            return acc

        acc = lax.fori_loop(0, nblk, body, jnp.zeros((COUNT_ROWS, T), I16))
        return jnp.sum(acc.astype(jnp.int32), axis=0, keepdims=True)

    def count_ge(ref, trial):
        t = rows16(trial)
        return count_rows(lambda r0: ref[pl.ds(r0, COUNT_ROWS), :] >= t)

    def kth_largest(ref, k):
        zero = jnp.zeros((1, T), jnp.int32)
        v = jnp.where(count_ge(ref, zero) >= k, zero, jnp.int32(I16_MIN))

        def bit_step(i, v):
            trial = v + lax.shift_left(jnp.int32(1), jnp.int32(14) - i)
            return jnp.where(count_ge(ref, trial) >= k, trial, v)

        return lax.fori_loop(0, 15, bit_step, v)

    hi_tau = jnp.maximum(kth_largest(hi_ref, top_k), jnp.int32(I16_MIN + 1))
    n_hi_ge = count_ge(hi_ref, hi_tau)
    n_hi_gt = jnp.where(hi_tau == I16_MAX, 0, count_ge(hi_ref, hi_tau + 1))
    need_lo = top_k - n_hi_gt

    def bucket_lo(kb, c):
        base = pl.multiple_of(kb * T, T)
        ht = rows16(hi_tau)
        for j in range(T // COUNT_ROWS):
            rows = pl.ds(base + j * COUNT_ROWS, COUNT_ROWS)
            lo_ref[rows, :] = jnp.where(hi_ref[rows, :] == ht, lo_ref[rows, :], jnp.int16(I16_MIN))
        return c

    lax.fori_loop(0, nblk, bucket_lo, 0)
    lo_tau = kth_largest(lo_ref, need_lo)

    n_lo_gt = jnp.where(lo_tau == I16_MAX, 0, count_ge(lo_ref, lo_tau + 1))
    n_lo_ge = jnp.where(lo_tau == I16_MIN, n_hi_ge - n_hi_gt, count_ge(lo_ref, lo_tau))
    need = need_lo - n_lo_gt
    excess = (n_lo_ge - n_lo_gt) > need

    @pl.when(jnp.max(excess.astype(jnp.int32)) > 0)
    def _():
        row_iota = lax.broadcasted_iota(jnp.int32, (COUNT_ROWS, T), 0)
        ht = rows16(hi_tau)
        lt = rows16(lo_tau)

        def is_tie(r0):
            rows = pl.ds(r0, COUNT_ROWS)
            return (hi_ref[rows, :] == ht) & (lo_ref[rows, :] == lt)

        def count_ties_upto(cut):
            cut_b = jnp.broadcast_to(cut, (COUNT_ROWS, T))
            return count_rows(lambda r0: is_tie(r0) & (row_iota + r0 <= cut_b))

        n_bits = max(1, int(hi_ref.shape[0] - 1).bit_length())

        def idx_step(i, cut):
            trial = cut - lax.shift_left(jnp.int32(1), jnp.int32(n_bits - 1) - i)
            return jnp.where(count_ties_upto(trial) >= need, trial, cut)

        cut = lax.fori_loop(0, n_bits, idx_step, jnp.full((1, T), 2 ** n_bits - 1, jnp.int32))
        cut = jnp.where(excess, cut, jnp.int32(2 ** 31 - 1))
        cut_b = jnp.broadcast_to(cut, (COUNT_ROWS, T))

        def demote(kb, c):
            base = pl.multiple_of(kb * T, T)
            for j in range(T // COUNT_ROWS):
                r0 = base + j * COUNT_ROWS
                rows = pl.ds(r0, COUNT_ROWS)
                drop = is_tie(r0) & (row_iota + r0 > cut_b)
                hi_ref[rows, :] = jnp.where(drop, jnp.int16(I16_MIN), hi_ref[rows, :])
            return c

        lax.fori_loop(0, nblk, demote, 0)

    m_ref[...] = jnp.full(m_ref.shape, -jnp.inf, F32)
    acc_ref[...] = jnp.zeros_like(acc_ref)

    def mask_add(j):
        rows = pl.ds(pl.multiple_of(tile_of(j) * T, T), T)
        hi_j = jnp.where(j < nblk, hi_tau, jnp.int32(I16_MAX))
        ht = jnp.broadcast_to(hi_j.astype(I16), (T, T))
        lt = jnp.broadcast_to(lo_tau.astype(I16), (T, T))
        hi = hi_ref[rows, :]
        one, zero = jnp.int16(1), jnp.int16(0)
        keep = jnp.where(hi > ht, one, jnp.where(hi == ht, jnp.where(lo_ref[rows, :] >= lt, one, zero), zero))
        return jnp.where(keep.astype(jnp.int32) > 0, 0.0, -jnp.inf).astype(F32)

    def logits(j, slot, kind, h, madd):
        hs = slice(h * HEAD_DIM, (h + 1) * HEAD_DIM)
        s = jnp.dot(kbuf[lax.rem(j, KV_SLOTS), :, hs], qT_ref[hs, :], preferred_element_type=F32) + madd
        if kind == 1:
            s = s + tz_ref[1, h]
        elif kind == 2:
            s = s + tz_ref[0, h]
        s_refs[slot][h] = s
        smax_refs[slot][h:h + 1, :] = jnp.max(s, axis=0, keepdims=True)

    def softmax(slot, h):
        s = s_refs[slot][h]
        m_old = m_ref[h:h + 1, :]
        m_new = jnp.maximum(m_old, smax_refs[slot][h:h + 1, :])
        m_safe = jnp.where(m_new == -jnp.inf, 0.0, m_new)
        p = jnp.exp2(s - m_safe)
        alpha = jnp.exp2(m_old - m_safe)
        m_ref[h:h + 1, :] = m_new
        alpha_refs[slot][h:h + 1, :] = alpha
        p_refs[slot][h] = p.astype(MXU_DTYPE)

    def weighted_values(j, slot, h):
        hv = slice(h * V_ROWS, (h + 1) * V_ROWS)
        pv = jnp.dot(vbuf[lax.rem(j, KV_SLOTS), hv, :], p_refs[slot][h], preferred_element_type=F32)
        acc_ref[hv, :] = alpha_refs[slot][h:h + 1, :] * acc_ref[hv, :] + pv

    def position(j, slot, kind, has_prev=True, has_next=True):
        if has_prev:
            v_copy(j - 1).wait()
        if has_next:
            k_copy(j + 1).wait()

            @pl.when(j + KV_SLOTS < npos)
            def _():
                k_copy(j + KV_SLOTS).start()

            @pl.when(j + KV_SLOTS - 2 < npos)
            def _():
                v_copy(j + KV_SLOTS - 2).start()

            madd = mask_add(j + 1)
        next_kind = 1 if kind == 2 else 0
        for h in range(N_HEADS):
            softmax(slot, h)
            if has_prev:
                weighted_values(j - 1, 1 - slot, h)
            if has_next:
                logits(j + 1, 1 - slot, next_kind, h, madd)

    k_copy(0).wait()
    madd0 = mask_add(0)
    for h in range(N_HEADS):
        logits(0, 0, 2, h, madd0)
    position(0, 0, 2, has_prev=False)
    position(1, 1, 1)

    def far_pair(t, c):
        j = 2 + 2 * t
        position(j, 0, 0)
        position(j + 1, 1, 0)
        return c

    lax.fori_loop(0, n_pairs - 1, far_pair, 0)
    position(npos - 2, 0, 0)
    position(npos - 1, 1, 0, has_next=False)
    v_copy(npos - 1).wait()
    for h in range(N_HEADS):
        weighted_values(npos - 1, 1, h)

    for h in range(N_HEADS):
        hs = slice(h * HEAD_DIM, (h + 1) * HEAD_DIM)
        out = acc_ref[h * V_ROWS:h * V_ROWS + HEAD_DIM, :] / acc_ref[h * V_ROWS + HEAD_DIM:h * V_ROWS + HEAD_DIM + 1, :]
        yT_ref[hs, :] = out * jax.nn.sigmoid(gaT_ref[hs, :])


def _attention(qiT, wiT, ki, qT, gaT, tab, kn, vT, top_k):
    B, D, Lp = qT.shape
    T = SEQ_TILE
    tr_spec = lambda n: pl.BlockSpec((None, n, T), lambda b, i: (b, 0, i))
    return pl.pallas_call(
        functools.partial(_attn_kernel, top_k),
        grid=(B, Lp // T),
        in_specs=[tr_spec(IDX_HEADS * IDX_PAD), tr_spec(WI_ROWS),
                  pl.BlockSpec((None, Lp, IDX_PAD), lambda b, i: (b, 0, 0), pipeline_mode=pl.Buffered(1)),
                  tr_spec(D), tr_spec(D),
                  pl.BlockSpec((N_HEADS, 2 * T), lambda b, i: (0, 0)),
                  pl.BlockSpec(memory_space=pl.ANY), pl.BlockSpec(memory_space=pl.ANY)],
        out_specs=tr_spec(D),
        out_shape=jax.ShapeDtypeStruct((B, D, Lp), F32),
        scratch_shapes=[pltpu.VMEM((Lp, T), I16), pltpu.VMEM((Lp, T), I16),
                        pltpu.VMEM((KV_SLOTS, T, D), MXU_DTYPE),
                        pltpu.VMEM((KV_SLOTS, N_HEADS * V_ROWS, T), MXU_DTYPE),
                        pltpu.SemaphoreType.DMA((2, KV_SLOTS)),
                        pltpu.VMEM((N_HEADS, T), F32),
                        pltpu.VMEM((N_HEADS * V_ROWS, T), F32),
                        pltpu.VMEM((2, N_HEADS, T, T), F32),
                        pltpu.VMEM((N_HEADS, T, T), F32), pltpu.VMEM((N_HEADS, T, T), F32),
                        pltpu.VMEM((N_HEADS, T, T), MXU_DTYPE), pltpu.VMEM((N_HEADS, T, T), MXU_DTYPE),
                        pltpu.VMEM((N_HEADS, T), F32), pltpu.VMEM((N_HEADS, T), F32),
                        pltpu.VMEM((N_HEADS, T), F32), pltpu.VMEM((N_HEADS, T), F32),
                        pltpu.VMEM((IDX_HEADS, T, T), F32), pltpu.VMEM((IDX_HEADS, T, T), F32)],
        compiler_params=pltpu.CompilerParams(dimension_semantics=("arbitrary", "arbitrary"),
                                             vmem_limit_bytes=VMEM_LIMIT_BYTES),
        name="attn",
    )(qiT, wiT, ki, qT, gaT, tab, kn, vT)


def _out_kernel(last, h_ref, yr_ref, yT_ref, wo_ref, g2_ref, w1_ref, w2_ref, fg_ref, o_ref):
    y = yr_ref[...] + yT_ref[...].T
    h1 = h_ref[...] + jnp.dot(y.astype(MXU_DTYPE), wo_ref[...], preferred_element_type=F32)
    hn = _rms(h1, g2_ref[...]).astype(MXU_DTYPE)
    a = jnp.dot(hn, w1_ref[...], preferred_element_type=F32)
    a = jnp.square(jnp.maximum(a, 0.0)).astype(MXU_DTYPE)
    h2 = h1 + jnp.dot(a, w2_ref[...], preferred_element_type=F32)
    if last:
        h2 = _rms(h2, fg_ref[...])
    o_ref[...] = h2


def _out_mlp(h, yr, yT, wo, g2, w1, w2, fg, last):
    B, Lp, D = h.shape
    T = SEQ_TILE
    row_spec = pl.BlockSpec((None, T, D), lambda b, i: (b, i, 0))
    const = lambda shape: pl.BlockSpec(shape, lambda b, i: (0,) * len(shape), pipeline_mode=pl.Buffered(1))
    return pl.pallas_call(
        functools.partial(_out_kernel, last),
        grid=(B, Lp // T),
        in_specs=[row_spec, row_spec, pl.BlockSpec((None, D, T), lambda b, i: (b, 0, i)),
                  const((D, D)), const((1, D)), const((D, D_FF)), const((D_FF, D)), const((1, D))],
        out_specs=row_spec,
        out_shape=jax.ShapeDtypeStruct((B, Lp, D), F32),
        compiler_params=pltpu.CompilerParams(dimension_semantics=("arbitrary", "arbitrary"),
                                             vmem_limit_bytes=VMEM_LIMIT_BYTES),
        name="out_mlp",
    )(h, yr, yT, wo, g2, w1, w2, fg)


def _t5_bucket_table(n):
    max_exact = N_BUCKETS // 2
    d = np.arange(n)
    df = np.maximum(d, 1).astype(np.float64)
    large = max_exact + (np.log(df / max_exact) / math.log(MAX_DISTANCE / max_exact)
                         * (N_BUCKETS - max_exact)).astype(np.int32)
    large = np.minimum(large, N_BUCKETS - 1)
    return np.where(d < max_exact, d, large).astype(np.int32)


def _bias_table(rel_bias):
    T = SEQ_TILE
    buckets = _t5_bucket_table(2 * T)
    assert np.all(buckets[T + 1:] == N_BUCKETS - 1)
    table = rel_bias[buckets] - rel_bias[N_BUCKETS - 1][None, :]
    return (table * LOG2E).T.astype(F32)


def _layer_weights(w_in):
    offs = np.concatenate([[0], np.cumsum(IN_SIZES)])
    parts = [w_in[:, int(offs[i]):int(offs[i + 1])] for i in range(len(IN_SIZES))]
    w_ug, w_ur, w_q, w_k, w_v, w_qi, w_ki, w_wi, w_gr, w_ga = parts
    D = w_in.shape[0]
    w_ki = jnp.pad(w_ki, ((0, 0), (0, IDX_PAD - IDX_DIM)))
    wn = jnp.concatenate([w_ug, w_ur, w_gr, w_k, w_ki], axis=1).astype(MXU_DTYPE)
    w_qi = jnp.pad(w_qi.reshape(D, IDX_HEADS, IDX_DIM), ((0, 0), (0, 0), (0, IDX_PAD - IDX_DIM)))
    w_qi = w_qi.reshape(D, IDX_HEADS * IDX_PAD)
    w_wi = jnp.pad(w_wi, ((0, 0), (0, WI_ROWS - IDX_HEADS)))
    scale = HEAD_DIM ** -0.5 * LOG2E
    wt = jnp.concatenate([w_q * scale, w_v, w_ga, w_qi, w_wi], axis=1).T.astype(MXU_DTYPE)
    return wn, wt


def kernel(x, norm1_g, w_in, conv_w, conv_b, w_rg_a, b_rg_a, w_rg_x, b_rg_x, lru_lambda, w_out, norm2_g,
           w_mlp1, w_mlp2, rel_bias, meta_tokens, final_g):
    B, S, D = x.shape
    depth = w_in.shape[0]
    L = S + N_META
    T = SEQ_TILE
    Lp = -(-L // T) * T
    top_k = min(TOP_K_MAX, L // 4)
    meta = jnp.broadcast_to(meta_tokens[None].astype(x.dtype), (B, N_META, D))
    h = jnp.concatenate([meta, x, jnp.zeros((B, Lp - L, D), x.dtype)], axis=1)
    tab = _bias_table(rel_bias)
    row = lambda v: v.reshape(1, -1)
    for l in range(depth):
        wn, wt = _layer_weights(w_in[l])
        ug, ur, gr, kn, ki, qT, vT, gaT, qiT, wiT = _proj(h, row(norm1_g[l]), wn, wt)
        yr = _rglru(ur, ug, gr, conv_w[l], row(conv_b[l]), w_rg_a[l].astype(MXU_DTYPE), row(b_rg_a[l]),
                    w_rg_x[l].astype(MXU_DTYPE), row(b_rg_x[l]), row(lru_lambda[l]))
        yT = _attention(qiT, wiT, ki, qT, gaT, tab, kn, vT, top_k)
        h = _out_mlp(h, yr, yT, w_out[l].astype(MXU_DTYPE), row(norm2_g[l]), w_mlp1[l].astype(MXU_DTYPE),
                     w_mlp2[l].astype(MXU_DTYPE), row(final_g), last=(l == depth - 1))
    return h[:, N_META:L]
```

```python
import functools
import math

import numpy as np
import jax
import jax.numpy as jnp
from jax import lax
from jax.experimental import pallas as pl
from jax.experimental.pallas import tpu as pltpu

D_MODEL = 1024
N_META = 16
D_RNN = D_MODEL
RG_BLOCKS = 4
RG_BLOCK_W = D_RNN // RG_BLOCKS
CONV_W = 4
LRU_C = 8.0
N_HEADS = 8
HEAD_DIM = D_MODEL // N_HEADS
IDX_HEADS = 8
IDX_DIM = 64
TOP_K_MAX = 256
N_BUCKETS = 32
MAX_DISTANCE = 128
D_FF = 4 * D_MODEL
NORM_EPS = 1e-6
IN_SIZES = (D_RNN, D_RNN, D_MODEL, D_MODEL, D_MODEL, IDX_HEADS * IDX_DIM, IDX_DIM, IDX_HEADS, D_MODEL, D_MODEL)

LANES = 128
SUBLANES = 8
VMEM_LIMIT_BYTES = 56 * 1024 * 1024

SEQ_TILE = 256
IDX_PAD = LANES
WI_ROWS = 16
COUNT_ROWS = 64
KV_SLOTS = 6
V_ROWS = HEAD_DIM + 16

MXU_DTYPE = jnp.bfloat16
F32 = jnp.float32
I16 = jnp.int16
I16_MIN = -(2 ** 15)
I16_MAX = 2 ** 15 - 1
LOG2E = math.log2(math.e)


def _rms(x, g):
    ms = jnp.mean(x * x, axis=-1, keepdims=True)
    return x * lax.rsqrt(ms + NORM_EPS) * g


N_NAT = 4 * D_MODEL + IDX_PAD
N_TR = 3 * D_MODEL + IDX_HEADS * IDX_PAD + WI_ROWS


def _proj_kernel(h_ref, g_ref, wn_ref, wt_ref,
                 ug_ref, ur_ref, gr_ref, k_ref, ki_ref,
                 qT_ref, vT_ref, gaT_ref, qiT_ref, wiT_ref):
    xn = _rms(h_ref[...], g_ref[...]).astype(MXU_DTYPE)

    def nat(c0, n):
        return jnp.dot(xn, wn_ref[:, c0:c0 + n], preferred_element_type=F32)

    def tr(r0, n):
        return lax.dot_general(wt_ref[r0:r0 + n, :], xn, (((1,), (1,)), ((), ())),
                               preferred_element_type=F32)

    ug_ref[...] = nat(0, D_MODEL)
    ur_ref[...] = nat(D_MODEL, D_MODEL)
    gr_ref[...] = nat(2 * D_MODEL, D_MODEL)
    k_ref[...] = nat(3 * D_MODEL, D_MODEL).astype(k_ref.dtype)
    ki_ref[...] = nat(4 * D_MODEL, IDX_PAD).astype(ki_ref.dtype)
    qT_ref[...] = tr(0, D_MODEL).astype(qT_ref.dtype)
    vt = tr(D_MODEL, D_MODEL).astype(vT_ref.dtype)
    for h in range(N_HEADS):
        vT_ref[h * V_ROWS:h * V_ROWS + HEAD_DIM, :] = vt[h * HEAD_DIM:(h + 1) * HEAD_DIM, :]
        vT_ref[h * V_ROWS + HEAD_DIM:(h + 1) * V_ROWS, :] = jnp.ones((V_ROWS - HEAD_DIM, vt.shape[1]), vT_ref.dtype)
    gaT_ref[...] = tr(2 * D_MODEL, D_MODEL)
    qiT_ref[...] = tr(3 * D_MODEL, IDX_HEADS * IDX_PAD).astype(qiT_ref.dtype)
    wiT_ref[...] = tr(3 * D_MODEL + IDX_HEADS * IDX_PAD, WI_ROWS)


def _proj(h, g1, wn, wt):
    B, Lp, D = h.shape
    T = SEQ_TILE
    nat_spec = lambda n: pl.BlockSpec((None, T, n), lambda b, i: (b, i, 0))
    tr_spec = lambda n: pl.BlockSpec((None, n, T), lambda b, i: (b, 0, i))
    const = lambda shape: pl.BlockSpec(shape, lambda b, i: (0,) * len(shape), pipeline_mode=pl.Buffered(1))
    sds = jax.ShapeDtypeStruct
    return pl.pallas_call(
        _proj_kernel,
        grid=(B, Lp // T),
        in_specs=[nat_spec(D), const((1, D)), const((D, N_NAT)), const((N_TR, D))],
        out_specs=[nat_spec(D), nat_spec(D), nat_spec(D), nat_spec(D), nat_spec(IDX_PAD),
                   tr_spec(D), pl.BlockSpec((None, None, N_HEADS * V_ROWS, T), lambda b, i: (b, i, 0, 0)),
                   tr_spec(D), tr_spec(IDX_HEADS * IDX_PAD),
                   tr_spec(WI_ROWS)],
        out_shape=[sds((B, Lp, D), F32), sds((B, Lp, D), F32), sds((B, Lp, D), F32),
                   sds((B, Lp, D), MXU_DTYPE), sds((B, Lp, IDX_PAD), MXU_DTYPE),
                   sds((B, D, Lp), MXU_DTYPE), sds((B, Lp // T, N_HEADS * V_ROWS, T), MXU_DTYPE),
                   sds((B, D, Lp), F32),
                   sds((B, IDX_HEADS * IDX_PAD, Lp), MXU_DTYPE), sds((B, WI_ROWS, Lp), F32)],
        compiler_params=pltpu.CompilerParams(dimension_semantics=("arbitrary", "arbitrary"),
                                             vmem_limit_bytes=VMEM_LIMIT_BYTES),
        name="proj",
    )(h, g1, wn, wt)


def _rglru_kernel(ur_ref, ug_ref, gr_ref, cw_ref, cb_ref, wa_ref, ba_ref, wx_ref, bx_ref, lam_ref,
                  y_ref, ext_ref, hst_ref):
    T = SEQ_TILE

    @pl.when(pl.program_id(1) == 0)
    def _():
        ext_ref[0:SUBLANES, :] = jnp.zeros((SUBLANES, D_RNN), F32)
        hst_ref[...] = jnp.zeros_like(hst_ref)

    x = ur_ref[...]
    ext_ref[SUBLANES:SUBLANES + T, :] = x
    base = SUBLANES - (CONV_W - 1)
    xc = cw_ref[0:1, :] * ext_ref[base:base + T, :]
    for j in range(1, CONV_W):
        xc = xc + cw_ref[j:j + 1, :] * ext_ref[base + j:base + j + T, :]
    xc = xc + cb_ref[...]
    ext_ref[0:SUBLANES, :] = x[T - SUBLANES:T, :]

    xcb = xc.astype(MXU_DTYPE)

    def block_diag(w_ref, b_ref):
        parts = [jnp.dot(xcb[:, g * RG_BLOCK_W:(g + 1) * RG_BLOCK_W], w_ref[g], preferred_element_type=F32)
                 for g in range(RG_BLOCKS)]
        return jnp.concatenate(parts, axis=-1) + b_ref[...]

    r = jax.nn.sigmoid(block_diag(wa_ref, ba_ref))
    gate_i = jax.nn.sigmoid(block_diag(wx_ref, bx_ref))
    z = -lam_ref[...]
    softplus = jnp.maximum(z, 0.0) + jnp.log1p(jnp.exp(-jnp.abs(z)))
    log_a = (-LRU_C) * r * softplus
    a = jnp.exp(log_a)
    u = jnp.sqrt(-jnp.tanh(log_a) * (a * a + 1.0)) * (gate_i * xc)

    row = lax.broadcasted_iota(jnp.int32, (T, D_RNN), 0)
    s = 1
    while s < T:
        a_prev = pltpu.roll(a, s, axis=0)
        u_prev = pltpu.roll(u, s, axis=0)
        valid = row >= s
        u = jnp.where(valid, a * u_prev + u, u)
        a = jnp.where(valid, a * a_prev, a)
        s *= 2
    h = a * hst_ref[...] + u
    hst_ref[...] = h[T - 1:T, :]
    y_ref[...] = h * jax.nn.gelu(ug_ref[...]) * jax.nn.sigmoid(gr_ref[...])


def _rglru(ur, ug, gr, cw, cb, wa, ba, wx, bx, lam):
    B, Lp, D = ur.shape
    T = SEQ_TILE
    row_spec = pl.BlockSpec((None, T, D), lambda b, i: (b, i, 0))
    const = lambda shape: pl.BlockSpec(shape, lambda b, i: (0,) * len(shape))
    return pl.pallas_call(
        _rglru_kernel,
        grid=(B, Lp // T),
        in_specs=[row_spec, row_spec, row_spec, const((CONV_W, D)), const((1, D)),
                  const((RG_BLOCKS, RG_BLOCK_W, RG_BLOCK_W)), const((1, D)),
                  const((RG_BLOCKS, RG_BLOCK_W, RG_BLOCK_W)), const((1, D)), const((1, D))],
        out_specs=row_spec,
        out_shape=jax.ShapeDtypeStruct((B, Lp, D), F32),
        scratch_shapes=[pltpu.VMEM((SUBLANES + T, D), F32), pltpu.VMEM((1, D), F32)],
        compiler_params=pltpu.CompilerParams(dimension_semantics=("arbitrary", "arbitrary"),
                                             vmem_limit_bytes=VMEM_LIMIT_BYTES),
        name="rglru",
    )(ur, ug, gr, cw, cb, wa, ba, wx, bx, lam)


def _to_sort_key(s):
    bits = lax.bitcast_convert_type(s, jnp.int32)
    return bits ^ ((bits >> 31) & jnp.int32(0x7FFFFFFF))


def _attn_kernel(top_k, qiT_ref, wiT_ref, ki_ref, qT_ref, gaT_ref, tab_ref, kn_hbm, vT_hbm,
                 yT_ref,
                 hi_ref, lo_ref, kbuf, vbuf, sem, m_ref, acc_ref, tz_ref,
                 s0_ref, s1_ref, p0_ref, p1_ref, alpha0_ref, alpha1_ref, smax0_ref, smax1_ref, si0_ref, si1_ref):
    T = SEQ_TILE
    s_refs, p_refs, alpha_refs = (s0_ref, s1_ref), (p0_ref, p1_ref), (alpha0_ref, alpha1_ref)
    smax_refs, si_refs = (smax0_ref, smax1_ref), (si0_ref, si1_ref)
    b = pl.program_id(0)
    qb = pl.program_id(1)
    nblk = qb + 1

    n_pairs = jnp.maximum((nblk - 1) // 2, 1)
    npos = 2 + 2 * n_pairs

    def tile_of(j):
        return jnp.maximum(nblk - 1 - j, 0)

    def k_copy(j):
        slot = lax.rem(j, KV_SLOTS)
        return pltpu.make_async_copy(kn_hbm.at[b, pl.ds(tile_of(j) * T, T), :], kbuf.at[slot], sem.at[0, slot])

    def v_copy(j):
        slot = lax.rem(j, KV_SLOTS)
        return pltpu.make_async_copy(vT_hbm.at[b, tile_of(j)], vbuf.at[slot], sem.at[1, slot])

    for j in range(KV_SLOTS):
        @pl.when(j < npos)
        def _():
            k_copy(j).start()

    for j in range(KV_SLOTS - 2):
        @pl.when(j < npos)
        def _():
            v_copy(j).start()

    @pl.when((b == 0) & (qb == 0))
    def _():
        for h in range(N_HEADS):
            rows = jnp.broadcast_to(tab_ref[h:h + 1, :], (T, 2 * T))
            skew = pltpu.roll(rows, 0, axis=1, stride=1, stride_axis=0)
            tz_ref[0, h] = skew[:, 0:T]
            tz_ref[1, h] = skew[:, T:2 * T]

    n_even = nblk + lax.rem(nblk, 2)

    def idx_tile(c):
        return jnp.where(c < nblk, c, 0)

    def idx_dot(c, slot, h):
        kib = ki_ref[pl.ds(pl.multiple_of(idx_tile(c) * T, T), T), :]
        si_refs[slot][h] = jnp.dot(kib, qiT_ref[h * IDX_PAD:(h + 1) * IDX_PAD, :], preferred_element_type=F32)

    key_minus_query = lax.broadcasted_iota(jnp.int32, (T, T), 0) - lax.broadcasted_iota(jnp.int32, (T, T), 1)

    def idx_step(c, slot, has_next=True):
        acc = jnp.zeros((T, T), F32)
        for h in range(IDX_HEADS):
            if has_next:
                idx_dot(c + 1, 1 - slot, h)
            acc = acc + jnp.maximum(si_refs[slot][h], 0.0) * wiT_ref[h:h + 1, :]
        kb = idx_tile(c)
        key = _to_sort_key(acc)
        hi = key >> 16
        lo = ((key ^ jnp.int32(0x8000)) << 16) >> 16
        hi = jnp.where(key_minus_query <= (qb - kb) * T, hi, I16_MIN)
        rows = pl.ds(pl.multiple_of(kb * T, T), T)
        hi_ref[rows, :] = hi.astype(I16)
        lo_ref[rows, :] = lo.astype(I16)

    for h in range(IDX_HEADS):
        idx_dot(0, 0, h)

    def idx_pair(t, c):
        idx_step(2 * t, 0)
        idx_step(2 * t + 1, 1)
        return c

    lax.fori_loop(0, n_even // 2 - 1, idx_pair, 0)
    idx_step(n_even - 2, 0)
    idx_step(n_even - 1, 1, has_next=False)

    def rows16(v):
        return jnp.broadcast_to(v.astype(I16), (COUNT_ROWS, T))

    def count_rows(pred):
        def body(kb, acc):
            base = pl.multiple_of(kb * T, T)
            for j in range(T // COUNT_ROWS):
                acc = acc + jnp.where(pred(base + j * COUNT_ROWS), jnp.int16(1), jnp.int16(0))
            return acc

        acc = lax.fori_loop(0, nblk, body, jnp.zeros((COUNT_ROWS, T), I16))
        return jnp.sum(acc.astype(jnp.int32), axis=0, keepdims=True)

    def count_ge(ref, trial):
        t = rows16(trial)
        return count_rows(lambda r0: ref[pl.ds(r0, COUNT_ROWS), :] >= t)

    def kth_largest(ref, k):
        zero = jnp.zeros((1, T), jnp.int32)
        v = jnp.where(count_ge(ref, zero) >= k, zero, jnp.int32(I16_MIN))

        def bit_step(i, v):
            trial = v + lax.shift_left(jnp.int32(1), jnp.int32(14) - i)
            return jnp.where(count_ge(ref, trial) >= k, trial, v)

        return lax.fori_loop(0, 15, bit_step, v)

    hi_tau = jnp.maximum(kth_largest(hi_ref, top_k), jnp.int32(I16_MIN + 1))
    n_hi_ge = count_ge(hi_ref, hi_tau)
    n_hi_gt = jnp.where(hi_tau == I16_MAX, 0, count_ge(hi_ref, hi_tau + 1))
    need_lo = top_k - n_hi_gt

    def bucket_lo(kb, c):
        base = pl.multiple_of(kb * T, T)
        ht = rows16(hi_tau)
        for j in range(T // COUNT_ROWS):
            rows = pl.ds(base + j * COUNT_ROWS, COUNT_ROWS)
            lo_ref[rows, :] = jnp.where(hi_ref[rows, :] == ht, lo_ref[rows, :], jnp.int16(I16_MIN))
        return c

    lax.fori_loop(0, nblk, bucket_lo, 0)
    lo_tau = kth_largest(lo_ref, need_lo)

    n_lo_gt = jnp.where(lo_tau == I16_MAX, 0, count_ge(lo_ref, lo_tau + 1))
    n_lo_ge = jnp.where(lo_tau == I16_MIN, n_hi_ge - n_hi_gt, count_ge(lo_ref, lo_tau))
    need = need_lo - n_lo_gt
    excess = (n_lo_ge - n_lo_gt) > need

    @pl.when(jnp.max(excess.astype(jnp.int32)) > 0)
    def _():
        ht = jnp.broadcast_to(hi_tau.astype(I16), (T, T))
        lt = jnp.broadcast_to(lo_tau.astype(I16), (T, T))
        one, zero = jnp.asarray(1, MXU_DTYPE), jnp.asarray(0, MXU_DTYPE)
        lower = jnp.where(lax.broadcasted_iota(jnp.int32, (T, T), 0) >= lax.broadcasted_iota(jnp.int32, (T, T), 1),
                          1.0, 0.0).astype(MXU_DTYPE)
        keep_upto = jnp.where(excess, need, jnp.int32(2 ** 30)).astype(F32)

        def demote(kb, seen):
            rows = pl.ds(pl.multiple_of(kb * T, T), T)
            hi = hi_ref[rows, :]
            tie = jnp.where(hi == ht, jnp.where(lo_ref[rows, :] == lt, one, zero), zero)
            rank = jnp.dot(lower, tie, preferred_element_type=F32) + seen
            over = jnp.where(rank > keep_upto, 1.0, 0.0).astype(MXU_DTYPE)
            hi_ref[rows, :] = jnp.where(tie * over > zero, jnp.int16(I16_MIN), hi)
            return rank[T - 1:T, :]

        lax.fori_loop(0, nblk, demote, jnp.zeros((1, T), F32))

    m_ref[...] = jnp.full(m_ref.shape, -jnp.inf, F32)
    acc_ref[...] = jnp.zeros_like(acc_ref)

    def mask_add(j):
        rows = pl.ds(pl.multiple_of(tile_of(j) * T, T), T)
        hi_j = jnp.where(j < nblk, hi_tau, jnp.int32(I16_MAX))
        ht = jnp.broadcast_to(hi_j.astype(I16), (T, T))
        lt = jnp.broadcast_to(lo_tau.astype(I16), (T, T))
        hi = hi_ref[rows, :]
        one, zero = jnp.int16(1), jnp.int16(0)
        keep = jnp.where(hi > ht, one, jnp.where(hi == ht, jnp.where(lo_ref[rows, :] >= lt, one, zero), zero))
        return jnp.where(keep.astype(jnp.int32) > 0, 0.0, -jnp.inf).astype(F32)

    def logits(j, slot, kind, h, madd):
        hs = slice(h * HEAD_DIM, (h + 1) * HEAD_DIM)
        s = jnp.dot(kbuf[lax.rem(j, KV_SLOTS), :, hs], qT_ref[hs, :], preferred_element_type=F32) + madd
        if kind == 1:
            s = s + tz_ref[1, h]
        elif kind == 2:
            s = s + tz_ref[0, h]
        s_refs[slot][h] = s
        smax_refs[slot][h:h + 1, :] = jnp.max(s, axis=0, keepdims=True)

    def softmax(slot, h):
        s = s_refs[slot][h]
        m_old = m_ref[h:h + 1, :]
        m_new = jnp.maximum(m_old, smax_refs[slot][h:h + 1, :])
        m_safe = jnp.where(m_new == -jnp.inf, 0.0, m_new)
        p = jnp.exp2(s - m_safe)
        alpha = jnp.exp2(m_old - m_safe)
        m_ref[h:h + 1, :] = m_new
        alpha_refs[slot][h:h + 1, :] = alpha
        p_refs[slot][h] = p.astype(MXU_DTYPE)

    def weighted_values(j, slot, h):
        hv = slice(h * V_ROWS, (h + 1) * V_ROWS)
        pv = jnp.dot(vbuf[lax.rem(j, KV_SLOTS), hv, :], p_refs[slot][h], preferred_element_type=F32)
        acc_ref[hv, :] = alpha_refs[slot][h:h + 1, :] * acc_ref[hv, :] + pv

    def position(j, slot, kind, has_prev=True, has_next=True):
        if has_prev:
            v_copy(j - 1).wait()
        if has_next:
            k_copy(j + 1).wait()

            @pl.when(j + KV_SLOTS < npos)
            def _():
                k_copy(j + KV_SLOTS).start()

            @pl.when(j + KV_SLOTS - 2 < npos)
            def _():
                v_copy(j + KV_SLOTS - 2).start()

            madd = mask_add(j + 1)
        next_kind = 1 if kind == 2 else 0
        for h in range(N_HEADS):
            softmax(slot, h)
            if has_prev:
                weighted_values(j - 1, 1 - slot, h)
            if has_next:
                logits(j + 1, 1 - slot, next_kind, h, madd)

    k_copy(0).wait()
    madd0 = mask_add(0)
    for h in range(N_HEADS):
        logits(0, 0, 2, h, madd0)
    position(0, 0, 2, has_prev=False)
    position(1, 1, 1)

    def far_pair(t, c):
        j = 2 + 2 * t
        position(j, 0, 0)
        position(j + 1, 1, 0)
        return c

    lax.fori_loop(0, n_pairs - 1, far_pair, 0)
    position(npos - 2, 0, 0)
    position(npos - 1, 1, 0, has_next=False)
    v_copy(npos - 1).wait()
    for h in range(N_HEADS):
        weighted_values(npos - 1, 1, h)

    for h in range(N_HEADS):
        hs = slice(h * HEAD_DIM, (h + 1) * HEAD_DIM)
        out = acc_ref[h * V_ROWS:h * V_ROWS + HEAD_DIM, :] / acc_ref[h * V_ROWS + HEAD_DIM:h * V_ROWS + HEAD_DIM + 1, :]
        yT_ref[hs, :] = out * jax.nn.sigmoid(gaT_ref[hs, :])


def _attention(qiT, wiT, ki, qT, gaT, tab, kn, vT, top_k):
    B, D, Lp = qT.shape
    T = SEQ_TILE
    tr_spec = lambda n: pl.BlockSpec((None, n, T), lambda b, i: (b, 0, i))
    return pl.pallas_call(
        functools.partial(_attn_kernel, top_k),
        grid=(B, Lp // T),
        in_specs=[tr_spec(IDX_HEADS * IDX_PAD), tr_spec(WI_ROWS),
                  pl.BlockSpec((None, Lp, IDX_PAD), lambda b, i: (b, 0, 0), pipeline_mode=pl.Buffered(1)),
                  tr_spec(D), tr_spec(D),
                  pl.BlockSpec((N_HEADS, 2 * T), lambda b, i: (0, 0)),
                  pl.BlockSpec(memory_space=pl.ANY), pl.BlockSpec(memory_space=pl.ANY)],
        out_specs=tr_spec(D),
        out_shape=jax.ShapeDtypeStruct((B, D, Lp), F32),
        scratch_shapes=[pltpu.VMEM((Lp, T), I16), pltpu.VMEM((Lp, T), I16),
                        pltpu.VMEM((KV_SLOTS, T, D), MXU_DTYPE),
                        pltpu.VMEM((KV_SLOTS, N_HEADS * V_ROWS, T), MXU_DTYPE),
                        pltpu.SemaphoreType.DMA((2, KV_SLOTS)),
                        pltpu.VMEM((N_HEADS, T), F32),
                        pltpu.VMEM((N_HEADS * V_ROWS, T), F32),
                        pltpu.VMEM((2, N_HEADS, T, T), F32),
                        pltpu.VMEM((N_HEADS, T, T), F32), pltpu.VMEM((N_HEADS, T, T), F32),
                        pltpu.VMEM((N_HEADS, T, T), MXU_DTYPE), pltpu.VMEM((N_HEADS, T, T), MXU_DTYPE),
                        pltpu.VMEM((N_HEADS, T), F32), pltpu.VMEM((N_HEADS, T), F32),
                        pltpu.VMEM((N_HEADS, T), F32), pltpu.VMEM((N_HEADS, T), F32),
                        pltpu.VMEM((IDX_HEADS, T, T), F32), pltpu.VMEM((IDX_HEADS, T, T), F32)],
        compiler_params=pltpu.CompilerParams(dimension_semantics=("arbitrary", "arbitrary"),
                                             vmem_limit_bytes=VMEM_LIMIT_BYTES),
        name="attn",
    )(qiT, wiT, ki, qT, gaT, tab, kn, vT)


def _out_kernel(last, h_ref, yr_ref, yT_ref, wo_ref, g2_ref, w1_ref, w2_ref, fg_ref, o_ref):
    y = yr_ref[...] + yT_ref[...].T
    h1 = h_ref[...] + jnp.dot(y.astype(MXU_DTYPE), wo_ref[...], preferred_element_type=F32)
    hn = _rms(h1, g2_ref[...]).astype(MXU_DTYPE)
    a = jnp.dot(hn, w1_ref[...], preferred_element_type=F32)
    a = jnp.square(jnp.maximum(a, 0.0)).astype(MXU_DTYPE)
    h2 = h1 + jnp.dot(a, w2_ref[...], preferred_element_type=F32)
    if last:
        h2 = _rms(h2, fg_ref[...])
    o_ref[...] = h2


def _out_mlp(h, yr, yT, wo, g2, w1, w2, fg, last):
    B, Lp, D = h.shape
    T = SEQ_TILE
    row_spec = pl.BlockSpec((None, T, D), lambda b, i: (b, i, 0))
    const = lambda shape: pl.BlockSpec(shape, lambda b, i: (0,) * len(shape), pipeline_mode=pl.Buffered(1))
    return pl.pallas_call(
        functools.partial(_out_kernel, last),
        grid=(B, Lp // T),
        in_specs=[row_spec, row_spec, pl.BlockSpec((None, D, T), lambda b, i: (b, 0, i)),
                  const((D, D)), const((1, D)), const((D, D_FF)), const((D_FF, D)), const((1, D))],
        out_specs=row_spec,
        out_shape=jax.ShapeDtypeStruct((B, Lp, D), F32),
        compiler_params=pltpu.CompilerParams(dimension_semantics=("arbitrary", "arbitrary"),
                                             vmem_limit_bytes=VMEM_LIMIT_BYTES),
        name="out_mlp",
    )(h, yr, yT, wo, g2, w1, w2, fg)


def _t5_bucket_table(n):
    max_exact = N_BUCKETS // 2
    d = np.arange(n)
    df = np.maximum(d, 1).astype(np.float64)
    large = max_exact + (np.log(df / max_exact) / math.log(MAX_DISTANCE / max_exact)
                         * (N_BUCKETS - max_exact)).astype(np.int32)
    large = np.minimum(large, N_BUCKETS - 1)
    return np.where(d < max_exact, d, large).astype(np.int32)


def _bias_table(rel_bias):
    T = SEQ_TILE
    buckets = _t5_bucket_table(2 * T)
    assert np.all(buckets[T + 1:] == N_BUCKETS - 1)
    table = rel_bias[buckets] - rel_bias[N_BUCKETS - 1][None, :]
    return (table * LOG2E).T.astype(F32)


def _layer_weights(w_in):
    offs = np.concatenate([[0], np.cumsum(IN_SIZES)])
    parts = [w_in[:, int(offs[i]):int(offs[i + 1])] for i in range(len(IN_SIZES))]
    w_ug, w_ur, w_q, w_k, w_v, w_qi, w_ki, w_wi, w_gr, w_ga = parts
    D = w_in.shape[0]
    w_ki = jnp.pad(w_ki, ((0, 0), (0, IDX_PAD - IDX_DIM)))
    wn = jnp.concatenate([w_ug, w_ur, w_gr, w_k, w_ki], axis=1).astype(MXU_DTYPE)
    w_qi = jnp.pad(w_qi.reshape(D, IDX_HEADS, IDX_DIM), ((0, 0), (0, 0), (0, IDX_PAD - IDX_DIM)))
    w_qi = w_qi.reshape(D, IDX_HEADS * IDX_PAD)
    w_wi = jnp.pad(w_wi, ((0, 0), (0, WI_ROWS - IDX_HEADS)))
    scale = HEAD_DIM ** -0.5 * LOG2E
    wt = jnp.concatenate([w_q * scale, w_v, w_ga, w_qi, w_wi], axis=1).T.astype(MXU_DTYPE)
    return wn, wt


def kernel(x, norm1_g, w_in, conv_w, conv_b, w_rg_a, b_rg_a, w_rg_x, b_rg_x, lru_lambda, w_out, norm2_g,
           w_mlp1, w_mlp2, rel_bias, meta_tokens, final_g):
    B, S, D = x.shape
    depth = w_in.shape[0]
    L = S + N_META
    T = SEQ_TILE
    Lp = -(-L // T) * T
    top_k = min(TOP_K_MAX, L // 4)
    meta = jnp.broadcast_to(meta_tokens[None].astype(x.dtype), (B, N_META, D))
    h = jnp.concatenate([meta, x, jnp.zeros((B, Lp - L, D), x.dtype)], axis=1)
    tab = _bias_table(rel_bias)
    row = lambda v: v.reshape(1, -1)
    for l in range(depth):
        wn, wt = _layer_weights(w_in[l])
        ug, ur, gr, kn, ki, qT, vT, gaT, qiT, wiT = _proj(h, row(norm1_g[l]), wn, wt)
        yr = _rglru(ur, ug, gr, conv_w[l], row(conv_b[l]), w_rg_a[l].astype(MXU_DTYPE), row(b_rg_a[l]),
                    w_rg_x[l].astype(MXU_DTYPE), row(b_rg_x[l]), row(lru_lambda[l]))
        yT = _attention(qiT, wiT, ki, qT, gaT, tab, kn, vT, top_k)
        h = _out_mlp(h, yr, yT, w_out[l].astype(MXU_DTYPE), row(norm2_g[l]), w_mlp1[l].astype(MXU_DTYPE),
                     w_mlp2[l].astype(MXU_DTYPE), row(final_g), last=(l == depth - 1))
    return h[:, N_META:L]
```

```python
import functools
import math

import numpy as np
import jax
import jax.numpy as jnp
from jax import lax
from jax.experimental import pallas as pl
from jax.experimental.pallas import tpu as pltpu

D_MODEL = 1024
N_META = 16
D_RNN = D_MODEL
RG_BLOCKS = 4
RG_BLOCK_W = D_RNN // RG_BLOCKS
CONV_W = 4
LRU_C = 8.0
N_HEADS = 8
HEAD_DIM = D_MODEL // N_HEADS
IDX_HEADS = 8
IDX_DIM = 64
TOP_K_MAX = 256
N_BUCKETS = 32
MAX_DISTANCE = 128
D_FF = 4 * D_MODEL
NORM_EPS = 1e-6
IN_SIZES = (D_RNN, D_RNN, D_MODEL, D_MODEL, D_MODEL, IDX_HEADS * IDX_DIM, IDX_DIM, IDX_HEADS, D_MODEL, D_MODEL)

LANES = 128
SUBLANES = 8
VMEM_LIMIT_BYTES = 56 * 1024 * 1024

SEQ_TILE = 256
IDX_PAD = LANES
WI_ROWS = 16
COUNT_ROWS = 64
KV_SLOTS = 6
V_ROWS = HEAD_DIM + 16

MXU_DTYPE = jnp.bfloat16
F32 = jnp.float32
I16 = jnp.int16
I16_MIN = -(2 ** 15)
I16_MAX = 2 ** 15 - 1
LOG2E = math.log2(math.e)


def _rms(x, g):
    ms = jnp.mean(x * x, axis=-1, keepdims=True)
    return x * lax.rsqrt(ms + NORM_EPS) * g


N_NAT = 4 * D_MODEL + IDX_PAD
N_TR = 3 * D_MODEL + IDX_HEADS * IDX_PAD + WI_ROWS


def _proj_kernel(h_ref, g_ref, wn_ref, wt_ref,
                 ug_ref, ur_ref, gr_ref, k_ref, ki_ref,
                 qT_ref, vT_ref, gaT_ref, qiT_ref, wiT_ref):
    xn = _rms(h_ref[...], g_ref[...]).astype(MXU_DTYPE)

    def nat(c0, n):
        return jnp.dot(xn, wn_ref[:, c0:c0 + n], preferred_element_type=F32)

    def tr(r0, n):
        return lax.dot_general(wt_ref[r0:r0 + n, :], xn, (((1,), (1,)), ((), ())),
                               preferred_element_type=F32)

    ug_ref[...] = nat(0, D_MODEL)
    ur_ref[...] = nat(D_MODEL, D_MODEL)
    gr_ref[...] = nat(2 * D_MODEL, D_MODEL)
    k_ref[...] = nat(3 * D_MODEL, D_MODEL).astype(k_ref.dtype)
    ki_ref[...] = nat(4 * D_MODEL, IDX_PAD).astype(ki_ref.dtype)
    qT_ref[...] = tr(0, D_MODEL).astype(qT_ref.dtype)
    vt = tr(D_MODEL, D_MODEL).astype(vT_ref.dtype)
    for h in range(N_HEADS):
        vT_ref[h * V_ROWS:h * V_ROWS + HEAD_DIM, :] = vt[h * HEAD_DIM:(h + 1) * HEAD_DIM, :]
        vT_ref[h * V_ROWS + HEAD_DIM:(h + 1) * V_ROWS, :] = jnp.ones((V_ROWS - HEAD_DIM, vt.shape[1]), vT_ref.dtype)
    gaT_ref[...] = tr(2 * D_MODEL, D_MODEL)
    qiT_ref[...] = tr(3 * D_MODEL, IDX_HEADS * IDX_PAD).astype(qiT_ref.dtype)
    wiT_ref[...] = tr(3 * D_MODEL + IDX_HEADS * IDX_PAD, WI_ROWS)


def _proj(h, g1, wn, wt):
    B, Lp, D = h.shape
    T = SEQ_TILE
    nat_spec = lambda n: pl.BlockSpec((None, T, n), lambda b, i: (b, i, 0))
    tr_spec = lambda n: pl.BlockSpec((None, n, T), lambda b, i: (b, 0, i))
    const = lambda shape: pl.BlockSpec(shape, lambda b, i: (0,) * len(shape), pipeline_mode=pl.Buffered(1))
    sds = jax.ShapeDtypeStruct
    return pl.pallas_call(
        _proj_kernel,
        grid=(B, Lp // T),
        in_specs=[nat_spec(D), const((1, D)), const((D, N_NAT)), const((N_TR, D))],
        out_specs=[nat_spec(D), nat_spec(D), nat_spec(D), nat_spec(D), nat_spec(IDX_PAD),
                   tr_spec(D), pl.BlockSpec((None, None, N_HEADS * V_ROWS, T), lambda b, i: (b, i, 0, 0)),
                   tr_spec(D), tr_spec(IDX_HEADS * IDX_PAD),
                   tr_spec(WI_ROWS)],
        out_shape=[sds((B, Lp, D), F32), sds((B, Lp, D), F32), sds((B, Lp, D), F32),
                   sds((B, Lp, D), MXU_DTYPE), sds((B, Lp, IDX_PAD), MXU_DTYPE),
                   sds((B, D, Lp), MXU_DTYPE), sds((B, Lp // T, N_HEADS * V_ROWS, T), MXU_DTYPE),
                   sds((B, D, Lp), F32),
                   sds((B, IDX_HEADS * IDX_PAD, Lp), MXU_DTYPE), sds((B, WI_ROWS, Lp), F32)],
        compiler_params=pltpu.CompilerParams(dimension_semantics=("arbitrary", "arbitrary"),
                                             vmem_limit_bytes=VMEM_LIMIT_BYTES),
        name="proj",
    )(h, g1, wn, wt)


def _rglru_kernel(ur_ref, ug_ref, gr_ref, cw_ref, cb_ref, wa_ref, ba_ref, wx_ref, bx_ref, lam_ref,
                  y_ref, ext_ref, hst_ref):
    T = SEQ_TILE

    @pl.when(pl.program_id(1) == 0)
    def _():
        ext_ref[0:SUBLANES, :] = jnp.zeros((SUBLANES, D_RNN), F32)
        hst_ref[...] = jnp.zeros_like(hst_ref)

    x = ur_ref[...]
    ext_ref[SUBLANES:SUBLANES + T, :] = x
    base = SUBLANES - (CONV_W - 1)
    xc = cw_ref[0:1, :] * ext_ref[base:base + T, :]
    for j in range(1, CONV_W):
        xc = xc + cw_ref[j:j + 1, :] * ext_ref[base + j:base + j + T, :]
    xc = xc + cb_ref[...]
    ext_ref[0:SUBLANES, :] = x[T - SUBLANES:T, :]

    xcb = xc.astype(MXU_DTYPE)

    def block_diag(w_ref, b_ref):
        parts = [jnp.dot(xcb[:, g * RG_BLOCK_W:(g + 1) * RG_BLOCK_W], w_ref[g], preferred_element_type=F32)
                 for g in range(RG_BLOCKS)]
        return jnp.concatenate(parts, axis=-1) + b_ref[...]

    r = jax.nn.sigmoid(block_diag(wa_ref, ba_ref))
    gate_i = jax.nn.sigmoid(block_diag(wx_ref, bx_ref))
    z = -lam_ref[...]
    softplus = jnp.maximum(z, 0.0) + jnp.log1p(jnp.exp(-jnp.abs(z)))
    log_a = (-LRU_C) * r * softplus
    a = jnp.exp(log_a)
    u = jnp.sqrt(-jnp.tanh(log_a) * (a * a + 1.0)) * (gate_i * xc)

    row = lax.broadcasted_iota(jnp.int32, (T, D_RNN), 0)
    s = 1
    while s < T:
        a_prev = pltpu.roll(a, s, axis=0)
        u_prev = pltpu.roll(u, s, axis=0)
        valid = row >= s
        u = jnp.where(valid, a * u_prev + u, u)
        a = jnp.where(valid, a * a_prev, a)
        s *= 2
    h = a * hst_ref[...] + u
    hst_ref[...] = h[T - 1:T, :]
    y_ref[...] = h * jax.nn.gelu(ug_ref[...]) * jax.nn.sigmoid(gr_ref[...])


def _rglru(ur, ug, gr, cw, cb, wa, ba, wx, bx, lam):
    B, Lp, D = ur.shape
    T = SEQ_TILE
    row_spec = pl.BlockSpec((None, T, D), lambda b, i: (b, i, 0))
    const = lambda shape: pl.BlockSpec(shape, lambda b, i: (0,) * len(shape))
    return pl.pallas_call(
        _rglru_kernel,
        grid=(B, Lp // T),
        in_specs=[row_spec, row_spec, row_spec, const((CONV_W, D)), const((1, D)),
                  const((RG_BLOCKS, RG_BLOCK_W, RG_BLOCK_W)), const((1, D)),
                  const((RG_BLOCKS, RG_BLOCK_W, RG_BLOCK_W)), const((1, D)), const((1, D))],
        out_specs=row_spec,
        out_shape=jax.ShapeDtypeStruct((B, Lp, D), F32),
        scratch_shapes=[pltpu.VMEM((SUBLANES + T, D), F32), pltpu.VMEM((1, D), F32)],
        compiler_params=pltpu.CompilerParams(dimension_semantics=("arbitrary", "arbitrary"),
                                             vmem_limit_bytes=VMEM_LIMIT_BYTES),
        name="rglru",
    )(ur, ug, gr, cw, cb, wa, ba, wx, bx, lam)


def _to_sort_key(s):
    bits = lax.bitcast_convert_type(s, jnp.int32)
    return bits ^ ((bits >> 31) & jnp.int32(0x7FFFFFFF))


def _attn_kernel(top_k, qiT_ref, wiT_ref, ki_ref, qT_ref, gaT_ref, tab_ref, kn_hbm, vT_hbm,
                 yT_ref,
                 hi_ref, lo_ref, kbuf, vbuf, sem, m_ref, acc_ref, tz_ref,
                 s0_ref, s1_ref, p0_ref, p1_ref, alpha0_ref, alpha1_ref, smax0_ref, smax1_ref, si0_ref, si1_ref):
    T = SEQ_TILE
    s_refs, p_refs, alpha_refs = (s0_ref, s1_ref), (p0_ref, p1_ref), (alpha0_ref, alpha1_ref)
    smax_refs, si_refs = (smax0_ref, smax1_ref), (si0_ref, si1_ref)
    b = pl.program_id(0)
    qb = pl.program_id(1)
    nblk = qb + 1

    n_pairs = jnp.maximum((nblk - 1) // 2, 1)
    npos = 2 + 2 * n_pairs

    def tile_of(j):
        return jnp.maximum(nblk - 1 - j, 0)

    def k_copy(j):
        slot = lax.rem(j, KV_SLOTS)
        return pltpu.make_async_copy(kn_hbm.at[b, pl.ds(tile_of(j) * T, T), :], kbuf.at[slot], sem.at[0, slot])

    def v_copy(j):
        slot = lax.rem(j, KV_SLOTS)
        return pltpu.make_async_copy(vT_hbm.at[b, tile_of(j)], vbuf.at[slot], sem.at[1, slot])

    for j in range(KV_SLOTS):
        @pl.when(j < npos)
        def _():
            k_copy(j).start()

    for j in range(KV_SLOTS - 2):
        @pl.when(j < npos)
        def _():
            v_copy(j).start()

    @pl.when((b == 0) & (qb == 0))
    def _():
        for h in range(N_HEADS):
            rows = jnp.broadcast_to(tab_ref[h:h + 1, :], (T, 2 * T))
            skew = pltpu.roll(rows, 0, axis=1, stride=1, stride_axis=0)
            tz_ref[0, h] = skew[:, 0:T]
            tz_ref[1, h] = skew[:, T:2 * T]

    n_even = nblk + lax.rem(nblk, 2)

    def idx_tile(c):
        return jnp.where(c < nblk, c, 0)

    def idx_dot(c, slot, h):
        kib = ki_ref[pl.ds(pl.multiple_of(idx_tile(c) * T, T), T), :]
        si_refs[slot][h] = jnp.dot(kib, qiT_ref[h * IDX_PAD:(h + 1) * IDX_PAD, :], preferred_element_type=F32)

    key_minus_query = lax.broadcasted_iota(jnp.int32, (T, T), 0) - lax.broadcasted_iota(jnp.int32, (T, T), 1)

    def idx_step(c, slot, has_next=True):
        acc = jnp.zeros((T, T), F32)
        for h in range(IDX_HEADS):
            if has_next:
                idx_dot(c + 1, 1 - slot, h)
            acc = acc + jnp.maximum(si_refs[slot][h], 0.0) * wiT_ref[h:h + 1, :]
        kb = idx_tile(c)
        key = _to_sort_key(acc)
        hi = key >> 16
        lo = ((key ^ jnp.int32(0x8000)) << 16) >> 16
        hi = jnp.where(key_minus_query <= (qb - kb) * T, hi, I16_MIN)
        rows = pl.ds(pl.multiple_of(kb * T, T), T)
        hi_ref[rows, :] = hi.astype(I16)
        lo_ref[rows, :] = lo.astype(I16)

    for h in range(IDX_HEADS):
        idx_dot(0, 0, h)

    def idx_pair(t, c):
        idx_step(2 * t, 0)
        idx_step(2 * t + 1, 1)
        return c

    lax.fori_loop(0, n_even // 2 - 1, idx_pair, 0)
    idx_step(n_even - 2, 0)
    idx_step(n_even - 1, 1, has_next=False)

    def rows16(v):
        return jnp.broadcast_to(v.astype(I16), (COUNT_ROWS, T))

    def count_rows(pred):
        def body(kb, acc):
            base = pl.multiple_of(kb * T, T)
            for j in range(T // COUNT_ROWS):
                acc = acc + jnp.where(pred(base + j * COUNT_ROWS), jnp.int16(1), jnp.int16(0))
            return acc

        acc = lax.fori_loop(0, nblk, body, jnp.zeros((COUNT_ROWS, T), I16))
        return jnp.sum(acc.astype(jnp.int32), axis=0, keepdims=True)

    def count_ge(ref, trial):
        t = rows16(trial)
        return count_rows(lambda r0: ref[pl.ds(r0, COUNT_ROWS), :] >= t)

    def kth_largest(ref, k):
        zero = jnp.zeros((1, T), jnp.int32)
        c0 = count_ge(ref, zero)
        ok = c0 >= k
        start = (jnp.where(ok, zero, jnp.int32(I16_MIN)), jnp.where(ok, c0, nblk * T), jnp.where(ok, 0, c0))

        def bit_step(i, carry):
            v, n_ge, n_gt = carry
            trial = v + lax.shift_left(jnp.int32(1), jnp.int32(14) - i)
            c = count_ge(ref, trial)
            ok = c >= k
            return jnp.where(ok, trial, v), jnp.where(ok, c, n_ge), jnp.where(ok, n_gt, c)

        return lax.fori_loop(0, 15, bit_step, start)

    hi_raw, n_raw_ge, n_hi_gt = kth_largest(hi_ref, top_k)
    hi_tau = jnp.maximum(hi_raw, jnp.int32(I16_MIN + 1))
    n_hi_ge = jnp.where(hi_raw == I16_MIN, n_hi_gt, n_raw_ge)
    need_lo = top_k - n_hi_gt

    def bucket_lo(kb, c):
        base = pl.multiple_of(kb * T, T)
        ht = rows16(hi_tau)
        for j in range(T // COUNT_ROWS):
            rows = pl.ds(base + j * COUNT_ROWS, COUNT_ROWS)
            lo_ref[rows, :] = jnp.where(hi_ref[rows, :] == ht, lo_ref[rows, :], jnp.int16(I16_MIN))
        return c

    lax.fori_loop(0, nblk, bucket_lo, 0)
    lo_tau, n_lo_raw_ge, n_lo_gt = kth_largest(lo_ref, need_lo)

    n_lo_ge = jnp.where(lo_tau == I16_MIN, n_hi_ge - n_hi_gt, n_lo_raw_ge)
    need = need_lo - n_lo_gt
    excess = (n_lo_ge - n_lo_gt) > need

    @pl.when(jnp.max(excess.astype(jnp.int32)) > 0)
    def _():
        ht = jnp.broadcast_to(hi_tau.astype(I16), (T, T))
        lt = jnp.broadcast_to(lo_tau.astype(I16), (T, T))
        one, zero = jnp.asarray(1, MXU_DTYPE), jnp.asarray(0, MXU_DTYPE)
        lower = jnp.where(lax.broadcasted_iota(jnp.int32, (T, T), 0) >= lax.broadcasted_iota(jnp.int32, (T, T), 1),
                          1.0, 0.0).astype(MXU_DTYPE)
        keep_upto = jnp.where(excess, need, jnp.int32(2 ** 30)).astype(F32)

        def demote(kb, seen):
            rows = pl.ds(pl.multiple_of(kb * T, T), T)
            hi = hi_ref[rows, :]
            tie = jnp.where(hi == ht, jnp.where(lo_ref[rows, :] == lt, one, zero), zero)
            rank = jnp.dot(lower, tie, preferred_element_type=F32) + seen
            over = jnp.where(rank > keep_upto, 1.0, 0.0).astype(MXU_DTYPE)
            hi_ref[rows, :] = jnp.where(tie * over > zero, jnp.int16(I16_MIN), hi)
            return rank[T - 1:T, :]

        lax.fori_loop(0, nblk, demote, jnp.zeros((1, T), F32))

    m_ref[...] = jnp.full(m_ref.shape, -jnp.inf, F32)
    acc_ref[...] = jnp.zeros_like(acc_ref)

    def mask_add(j):
        rows = pl.ds(pl.multiple_of(tile_of(j) * T, T), T)
        hi_j = jnp.where(j < nblk, hi_tau, jnp.int32(I16_MAX))
        ht = jnp.broadcast_to(hi_j.astype(I16), (T, T))
        lt = jnp.broadcast_to(lo_tau.astype(I16), (T, T))
        hi = hi_ref[rows, :]
        one, zero = jnp.int16(1), jnp.int16(0)
        keep = jnp.where(hi > ht, one, jnp.where(hi == ht, jnp.where(lo_ref[rows, :] >= lt, one, zero), zero))
        return jnp.where(keep.astype(jnp.int32) > 0, 0.0, -jnp.inf).astype(F32)

    def logits(j, slot, kind, h, madd):
        hs = slice(h * HEAD_DIM, (h + 1) * HEAD_DIM)
        s = jnp.dot(kbuf[lax.rem(j, KV_SLOTS), :, hs], qT_ref[hs, :], preferred_element_type=F32) + madd
        if kind == 1:
            s = s + tz_ref[1, h]
        elif kind == 2:
            s = s + tz_ref[0, h]
        s_refs[slot][h] = s
        smax_refs[slot][h:h + 1, :] = jnp.max(s, axis=0, keepdims=True)

    def softmax(slot, h):
        s = s_refs[slot][h]
        m_old = m_ref[h:h + 1, :]
        m_new = jnp.maximum(m_old, smax_refs[slot][h:h + 1, :])
        m_safe = jnp.where(m_new == -jnp.inf, 0.0, m_new)
        p = jnp.exp2(s - m_safe)
        alpha = jnp.exp2(m_old - m_safe)
        m_ref[h:h + 1, :] = m_new
        alpha_refs[slot][h:h + 1, :] = alpha
        p_refs[slot][h] = p.astype(MXU_DTYPE)

    def weighted_values(j, slot, h):
        hv = slice(h * V_ROWS, (h + 1) * V_ROWS)
        pv = jnp.dot(vbuf[lax.rem(j, KV_SLOTS), hv, :], p_refs[slot][h], preferred_element_type=F32)
        acc_ref[hv, :] = alpha_refs[slot][h:h + 1, :] * acc_ref[hv, :] + pv

    def position(j, slot, kind, has_prev=True, has_next=True):
        if has_prev:
            v_copy(j - 1).wait()
        if has_next:
            k_copy(j + 1).wait()

            @pl.when(j + KV_SLOTS < npos)
            def _():
                k_copy(j + KV_SLOTS).start()

            @pl.when(j + KV_SLOTS - 2 < npos)
            def _():
                v_copy(j + KV_SLOTS - 2).start()

            madd = mask_add(j + 1)
        next_kind = 1 if kind == 2 else 0
        for h in range(N_HEADS):
            softmax(slot, h)
            if has_prev:
                weighted_values(j - 1, 1 - slot, h)
            if has_next:
                logits(j + 1, 1 - slot, next_kind, h, madd)

    k_copy(0).wait()
    madd0 = mask_add(0)
    for h in range(N_HEADS):
        logits(0, 0, 2, h, madd0)
    position(0, 0, 2, has_prev=False)
    position(1, 1, 1)

    def far_pair(t, c):
        j = 2 + 2 * t
        position(j, 0, 0)
        position(j + 1, 1, 0)
        return c

    lax.fori_loop(0, n_pairs - 1, far_pair, 0)
    position(npos - 2, 0, 0)
    position(npos - 1, 1, 0, has_next=False)
    v_copy(npos - 1).wait()
    for h in range(N_HEADS):
        weighted_values(npos - 1, 1, h)

    for h in range(N_HEADS):
        hs = slice(h * HEAD_DIM, (h + 1) * HEAD_DIM)
        out = acc_ref[h * V_ROWS:h * V_ROWS + HEAD_DIM, :] / acc_ref[h * V_ROWS + HEAD_DIM:h * V_ROWS + HEAD_DIM + 1, :]
        yT_ref[hs, :] = out * jax.nn.sigmoid(gaT_ref[hs, :])


def _attention(qiT, wiT, ki, qT, gaT, tab, kn, vT, top_k):
    B, D, Lp = qT.shape
    T = SEQ_TILE
    tr_spec = lambda n: pl.BlockSpec((None, n, T), lambda b, i: (b, 0, i))
    return pl.pallas_call(
        functools.partial(_attn_kernel, top_k),
        grid=(B, Lp // T),
        in_specs=[tr_spec(IDX_HEADS * IDX_PAD), tr_spec(WI_ROWS),
                  pl.BlockSpec((None, Lp, IDX_PAD), lambda b, i: (b, 0, 0), pipeline_mode=pl.Buffered(1)),
                  tr_spec(D), tr_spec(D),
                  pl.BlockSpec((N_HEADS, 2 * T), lambda b, i: (0, 0)),
                  pl.BlockSpec(memory_space=pl.ANY), pl.BlockSpec(memory_space=pl.ANY)],
        out_specs=tr_spec(D),
        out_shape=jax.ShapeDtypeStruct((B, D, Lp), F32),
        scratch_shapes=[pltpu.VMEM((Lp, T), I16), pltpu.VMEM((Lp, T), I16),
                        pltpu.VMEM((KV_SLOTS, T, D), MXU_DTYPE),
                        pltpu.VMEM((KV_SLOTS, N_HEADS * V_ROWS, T), MXU_DTYPE),
                        pltpu.SemaphoreType.DMA((2, KV_SLOTS)),
                        pltpu.VMEM((N_HEADS, T), F32),
                        pltpu.VMEM((N_HEADS * V_ROWS, T), F32),
                        pltpu.VMEM((2, N_HEADS, T, T), F32),
                        pltpu.VMEM((N_HEADS, T, T), F32), pltpu.VMEM((N_HEADS, T, T), F32),
                        pltpu.VMEM((N_HEADS, T, T), MXU_DTYPE), pltpu.VMEM((N_HEADS, T, T), MXU_DTYPE),
                        pltpu.VMEM((N_HEADS, T), F32), pltpu.VMEM((N_HEADS, T), F32),
                        pltpu.VMEM((N_HEADS, T), F32), pltpu.VMEM((N_HEADS, T), F32),
                        pltpu.VMEM((IDX_HEADS, T, T), F32), pltpu.VMEM((IDX_HEADS, T, T), F32)],
        compiler_params=pltpu.CompilerParams(dimension_semantics=("arbitrary", "arbitrary"),
                                             vmem_limit_bytes=VMEM_LIMIT_BYTES),
        name="attn",
    )(qiT, wiT, ki, qT, gaT, tab, kn, vT)


def _out_kernel(last, h_ref, yr_ref, yT_ref, wo_ref, g2_ref, w1_ref, w2_ref, fg_ref, o_ref):
    y = yr_ref[...] + yT_ref[...].T
    h1 = h_ref[...] + jnp.dot(y.astype(MXU_DTYPE), wo_ref[...], preferred_element_type=F32)
    hn = _rms(h1, g2_ref[...]).astype(MXU_DTYPE)
    a = jnp.dot(hn, w1_ref[...], preferred_element_type=F32)
    a = jnp.square(jnp.maximum(a, 0.0)).astype(MXU_DTYPE)
    h2 = h1 + jnp.dot(a, w2_ref[...], preferred_element_type=F32)
    if last:
        h2 = _rms(h2, fg_ref[...])
    o_ref[...] = h2


def _out_mlp(h, yr, yT, wo, g2, w1, w2, fg, last):
    B, Lp, D = h.shape
    T = SEQ_TILE
    row_spec = pl.BlockSpec((None, T, D), lambda b, i: (b, i, 0))
    const = lambda shape: pl.BlockSpec(shape, lambda b, i: (0,) * len(shape), pipeline_mode=pl.Buffered(1))
    return pl.pallas_call(
        functools.partial(_out_kernel, last),
        grid=(B, Lp // T),
        in_specs=[row_spec, row_spec, pl.BlockSpec((None, D, T), lambda b, i: (b, 0, i)),
                  const((D, D)), const((1, D)), const((D, D_FF)), const((D_FF, D)), const((1, D))],
        out_specs=row_spec,
        out_shape=jax.ShapeDtypeStruct((B, Lp, D), F32),
        compiler_params=pltpu.CompilerParams(dimension_semantics=("arbitrary", "arbitrary"),
                                             vmem_limit_bytes=VMEM_LIMIT_BYTES),
        name="out_mlp",
    )(h, yr, yT, wo, g2, w1, w2, fg)


def _t5_bucket_table(n):
    max_exact = N_BUCKETS // 2
    d = np.arange(n)
    df = np.maximum(d, 1).astype(np.float64)
    large = max_exact + (np.log(df / max_exact) / math.log(MAX_DISTANCE / max_exact)
                         * (N_BUCKETS - max_exact)).astype(np.int32)
    large = np.minimum(large, N_BUCKETS - 1)
    return np.where(d < max_exact, d, large).astype(np.int32)


def _bias_table(rel_bias):
    T = SEQ_TILE
    buckets = _t5_bucket_table(2 * T)
    assert np.all(buckets[T + 1:] == N_BUCKETS - 1)
    table = rel_bias[buckets] - rel_bias[N_BUCKETS - 1][None, :]
    return (table * LOG2E).T.astype(F32)


def _layer_weights(w_in):
    offs = np.concatenate([[0], np.cumsum(IN_SIZES)])
    parts = [w_in[:, int(offs[i]):int(offs[i + 1])] for i in range(len(IN_SIZES))]
    w_ug, w_ur, w_q, w_k, w_v, w_qi, w_ki, w_wi, w_gr, w_ga = parts
    D = w_in.shape[0]
    w_ki = jnp.pad(w_ki, ((0, 0), (0, IDX_PAD - IDX_DIM)))
    wn = jnp.concatenate([w_ug, w_ur, w_gr, w_k, w_ki], axis=1).astype(MXU_DTYPE)
    w_qi = jnp.pad(w_qi.reshape(D, IDX_HEADS, IDX_DIM), ((0, 0), (0, 0), (0, IDX_PAD - IDX_DIM)))
    w_qi = w_qi.reshape(D, IDX_HEADS * IDX_PAD)
    w_wi = jnp.pad(w_wi, ((0, 0), (0, WI_ROWS - IDX_HEADS)))
    scale = HEAD_DIM ** -0.5 * LOG2E
    wt = jnp.concatenate([w_q * scale, w_v, w_ga, w_qi, w_wi], axis=1).T.astype(MXU_DTYPE)
    return wn, wt


def kernel(x, norm1_g, w_in, conv_w, conv_b, w_rg_a, b_rg_a, w_rg_x, b_rg_x, lru_lambda, w_out, norm2_g,
           w_mlp1, w_mlp2, rel_bias, meta_tokens, final_g):
    B, S, D = x.shape
    depth = w_in.shape[0]
    L = S + N_META
    T = SEQ_TILE
    Lp = -(-L // T) * T
    top_k = min(TOP_K_MAX, L // 4)
    meta = jnp.broadcast_to(meta_tokens[None].astype(x.dtype), (B, N_META, D))
    h = jnp.concatenate([meta, x, jnp.zeros((B, Lp - L, D), x.dtype)], axis=1)
    tab = _bias_table(rel_bias)
    row = lambda v: v.reshape(1, -1)
    for l in range(depth):
        wn, wt = _layer_weights(w_in[l])
        ug, ur, gr, kn, ki, qT, vT, gaT, qiT, wiT = _proj(h, row(norm1_g[l]), wn, wt)
        yr = _rglru(ur, ug, gr, conv_w[l], row(conv_b[l]), w_rg_a[l].astype(MXU_DTYPE), row(b_rg_a[l]),
                    w_rg_x[l].astype(MXU_DTYPE), row(b_rg_x[l]), row(lru_lambda[l]))
        yT = _attention(qiT, wiT, ki, qT, gaT, tab, kn, vT, top_k)
        h = _out_mlp(h, yr, yT, w_out[l].astype(MXU_DTYPE), row(norm2_g[l]), w_mlp1[l].astype(MXU_DTYPE),
                     w_mlp2[l].astype(MXU_DTYPE), row(final_g), last=(l == depth - 1))
    return h[:, N_META:L]
```
